```python
import jax
import jax.numpy as jnp
from jax import lax
import numpy as np

D_MODEL = 1024
BATCH = 8
SEQ = 2048
DEPTH = 4
DEC_BATCH = 128
DEC_SEQ = 4
PAST_LEN = 2048
PAGE_SIZE = 128

HEAD_DIM = 64
ROPE_THETA = 10000.0
EPS = 1e-6
NEG = -1e30
A_HEADS = 6
A_KV_HEADS = 2
A_REP = A_HEADS // A_KV_HEADS
CMP_LEN = 32
CMP_STRIDE = 16
CMP_HIDDEN = 128
SEL_BLOCK = 64
SEL_TOP = 16
FORCE_BONUS = 1e4
WINDOW = 512
NSA_QBLOCK = 32
WIN_QBLOCK = 128
B_HEADS = 4
B_DK = 64
B_DV = 64
HGRN_CHUNK = 64
C_HEADS = 6
MOBA_BLOCK = 256
MOBA_TOP = 3
MOBA_QBLOCK = 16
D_FF = 2816
CONV_W = 3

A_WIDTH = A_HEADS * HEAD_DIM
A_KV_WIDTH = A_KV_HEADS * HEAD_DIM
B_WIDTH = B_HEADS * B_DV
C_WIDTH = C_HEADS * HEAD_DIM
MIX_WIDTH = A_WIDTH + B_WIDTH + C_WIDTH
IN_SPLITS = (A_WIDTH, A_KV_WIDTH, A_KV_WIDTH, A_KV_WIDTH, A_KV_WIDTH, A_KV_WIDTH, A_KV_WIDTH, 3 * A_HEADS,
             B_HEADS * B_DK, B_HEADS * B_DK, B_WIDTH, B_WIDTH,
             C_WIDTH, C_WIDTH, C_WIDTH)
N_IN = sum(IN_SPLITS)

kernel_name = 'hybrid_nsa_hgrn2_moba_step'


def rms_norm(x, g):
    xf = x.astype(jnp.float32)
    y = xf * lax.rsqrt(jnp.mean(xf * xf, axis=-1, keepdims=True) + EPS)
    return (y * g.astype(jnp.float32)).astype(x.dtype)


def rope(x, pos):
    half = x.shape[-1] // 2
    inv = ROPE_THETA ** (-jnp.arange(half, dtype=jnp.float32) / half)
    ang = pos.astype(jnp.float32)[:, None] * inv[None, :]
    shape = (1, pos.shape[0]) + (1,) * (x.ndim - 3) + (half,)
    cos = jnp.cos(ang).reshape(shape)
    sin = jnp.sin(ang).reshape(shape)
    xf = x.astype(jnp.float32)
    x1, x2 = xf[..., :half], xf[..., half:]
    return jnp.concatenate([x1 * cos - x2 * sin, x2 * cos + x1 * sin], axis=-1).astype(x.dtype)


def masked_softmax(s, mask):
    s = jnp.where(mask, s.astype(jnp.float32), NEG)
    m = jnp.max(s, axis=-1, keepdims=True)
    p = jnp.where(mask, jnp.exp(s - m), 0.0)
    return p / jnp.maximum(jnp.sum(p, axis=-1, keepdims=True), 1e-30)


def over_query_blocks(fn, block, *args):
    T = args[0].shape[1]
    if T <= block or T % block:
        return fn(*args)
    n = T // block

    def split(a):
        return jnp.moveaxis(a.reshape((a.shape[0], n, block) + a.shape[2:]), 1, 0)

    out = lax.map(lambda xs: fn(*xs), tuple(split(a) for a in args))
    out = jnp.moveaxis(out, 0, 1)
    return out.reshape((out.shape[0], T) + out.shape[3:])


def gather_pages(pool, page_table):
    rows = pool[page_table]
    return rows.reshape((rows.shape[0], rows.shape[1] * rows.shape[2]) + rows.shape[3:])


def gqa_attend(q, k, v, mask):
    s = jnp.einsum('btgrd,bsgd->bgrts', q, k) * (q.shape[-1] ** -0.5)
    p = masked_softmax(s, mask)
    return jnp.einsum('bgrts,bsgd->btgrd', p.astype(v.dtype), v)


def window_attend_banded(q, k, v):
    B, T, G, R, d = q.shape
    n = T // WIN_QBLOCK
    span = WIN_QBLOCK + WINDOW
    padw = ((0, 0), (WINDOW, 0), (0, 0), (0, 0))
    kp, vp = jnp.pad(k, padw), jnp.pad(v, padw)
    idx = (jnp.arange(n) * WIN_QBLOCK)[:, None] + jnp.arange(span)[None, :]
    kpos = idx - WINDOW
    qpos = jnp.arange(T).reshape(n, WIN_QBLOCK)
    mask = ((kpos[:, None, :] >= 0) & (kpos[:, None, :] <= qpos[:, :, None])
            & (qpos[:, :, None] - kpos[:, None, :] <= WINDOW))
    qb = q.reshape(B, n, WIN_QBLOCK, G, R, d)
    o = jax.vmap(gqa_attend, in_axes=(1, 1, 1, 0), out_axes=1)(qb, kp[:, idx], vp[:, idx], mask[:, None, None, None])
    return o.reshape(B, T, G, R, d)


def nsa_compress(k, pe, pool, w1, w2):
    L = k.shape[1]
    n_cmp = (L - CMP_LEN) // CMP_STRIDE + 1
    idx = (jnp.arange(n_cmp) * CMP_STRIDE)[:, None] + jnp.arange(CMP_LEN)[None, :]
    blocks = k[:, idx]
    pooled = jnp.einsum('bnpgd,pd->bngd', blocks, pool) + jnp.sum(pe * pool, axis=0)
    hid = jax.nn.silu(jnp.einsum('bngd,de->bnge', pooled, w1))
    return jnp.einsum('bnge,ed->bngd', hid, w2)


def nsa_mixer(q, cmp_kv, sel_kv, win_kv, win_pos, q_pos, gates, banded, pe, pool, w1, w2):
    B, T, G, R, d = q.shape
    L = cmp_kv.shape[1]
    scale = d ** -0.5
    q_rot = rope(q, q_pos)
    kc = nsa_compress(cmp_kv[:, :, 0], pe[0], pool[0], w1[0], w2[0])
    vc = nsa_compress(cmp_kv[:, :, 1], pe[1], pool[1], w1[1], w2[1])
    n_cmp = kc.shape[1]
    cmp_start = jnp.arange(n_cmp) * CMP_STRIDE
    cmp_mask = (cmp_start + CMP_LEN - 1)[None, :] <= q_pos[:, None]
    s = jnp.einsum('btgrd,bngd->bgrtn', q, kc) * scale
    p_cmp = masked_softmax(s, cmp_mask)
    o_cmp = jnp.einsum('bgrtn,bngd->btgrd', p_cmp.astype(vc.dtype), vc)
    n_sel = -(-L // SEL_BLOCK)
    sel_start = jnp.arange(n_sel) * SEL_BLOCK
    overlap = ((cmp_start[:, None] < sel_start[None, :] + SEL_BLOCK)
               & (cmp_start[:, None] + CMP_LEN > sel_start[None, :])).astype(jnp.float32)
    imp = jnp.einsum('bgrtn,ns->btgs', p_cmp, overlap)
    cur = q_pos // SEL_BLOCK
    blk = jnp.arange(n_sel)
    forced = (blk[None, :] == 0) | (blk[None, :] == cur[:, None]) | (blk[None, :] == cur[:, None] - 1)
    visible = sel_start[None, :] <= q_pos[:, None]
    score = jnp.where(visible[None, :, None], imp + jnp.where(forced, FORCE_BONUS, 0.0)[None, :, None], NEG)
    _, top = lax.top_k(score, min(SEL_TOP, n_sel))
    kvp = jnp.pad(sel_kv, ((0, 0), (0, n_sel * SEL_BLOCK - L), (0, 0), (0, 0), (0, 0)))
    kvb = kvp.reshape(B, n_sel, SEL_BLOCK, 2, G, d).transpose(3, 0, 4, 1, 2, 5)
    kbt, vbt = kvb[0], kvb[1]

    def sel_attend(q_c, top_c, pos_c):
        bi = jnp.arange(q_c.shape[0])[:, None, None, None]
        gi = jnp.arange(G)[None, None, :, None]
        kg = kbt[bi, gi, top_c]
        vg = vbt[bi, gi, top_c]
        kpos = top_c[..., None] * SEL_BLOCK + jnp.arange(SEL_BLOCK)
        mask = kpos <= pos_c[:, :, None, None, None]
        sc = jnp.einsum('bcgrd,bcgksd->bcgrks', q_c, kg) * scale
        sh = sc.shape
        p = masked_softmax(sc.reshape(sh[:4] + (-1,)), mask.reshape(mask.shape[:3] + (1, -1))).reshape(sh)
        return jnp.einsum('bcgrks,bcgksd->bcgrd', p.astype(vg.dtype), vg)

    o_sel = over_query_blocks(sel_attend, NSA_QBLOCK, q_rot, top, q_pos[None])
    if banded:
        o_win = window_attend_banded(q_rot, win_kv[:, :, 0], win_kv[:, :, 1])
    else:
        wmask = (win_pos[None, :] <= q_pos[:, None]) & (q_pos[:, None] - win_pos[None, :] <= WINDOW)
        o_win = gqa_attend(q_rot, win_kv[:, :, 0], win_kv[:, :, 1], wmask)
    g = jax.nn.sigmoid(gates.astype(jnp.float32)).reshape(B, T, G, R, 3).astype(q.dtype)
    o = g[..., 0:1] * o_cmp + g[..., 1:2] * o_sel + g[..., 2:3] * o_win
    return o.reshape(B, T, A_WIDTH)


def hgrn_lower_bounds(logits):
    p = jax.nn.softmax(logits.astype(jnp.float32), axis=0)
    return jnp.clip(jnp.cumsum(p, axis=0) - p[0:1], 0.0, 1.0)


def hgrn_scan(q, k, v, logf, s0, chunk):
    B, T, H, _ = q.shape
    n = T // chunk

    def to_chunks(a):
        return a.reshape(B, n, chunk, H, a.shape[-1]).transpose(1, 0, 3, 2, 4)

    causal = jnp.tril(jnp.ones((chunk, chunk), bool))

    def step(S, xs):
        qc, kc, vc, fc = xs
        b = jnp.cumsum(fc, axis=2)
        o_inter = jnp.einsum('bhck,bhkv->bhcv', qc * jnp.exp(b), S)
        rel = b[:, :, :, None, :] - b[:, :, None, :, :]
        decay = jnp.where(causal[:, :, None], jnp.exp(jnp.minimum(rel, 0.0)), 0.0)
        attn = jnp.einsum('bhtk,bhtsk,bhsk->bhts', qc, decay, kc)
        o = o_inter + jnp.einsum('bhts,bhsv->bhtv', attn, vc)
        b_last = b[:, :, -1:, :]
        S = jnp.exp(b_last[:, :, 0, :])[..., None] * S + jnp.einsum('bhsk,bhsv->bhkv', kc * jnp.exp(b_last - b), vc)
        return S, o

    S, o = lax.scan(step, s0, tuple(to_chunks(a) for a in (q, k, v, logf)))
    return o.transpose(1, 0, 3, 2, 4).reshape(B, T, H, -1), S


def hgrn_mixer(q, f, i, g, s0, lb, norm_g):
    B, T, _ = q.shape
    f32 = jnp.float32
    z = f.astype(f32).reshape(B, T, B_HEADS, B_DK)
    lb = lb.reshape(B_HEADS, B_DK)
    logf = jnp.logaddexp(jnp.log(lb), jnp.log1p(-lb) + jax.nn.log_sigmoid(z))
    key = (1.0 - lb) * jax.nn.sigmoid(-z)
    qf = jax.nn.silu(q.astype(f32)).reshape(B, T, B_HEADS, B_DK)
    v = i.astype(f32).reshape(B, T, B_HEADS, B_DV)
    chunk = HGRN_CHUNK if T % HGRN_CHUNK == 0 else T
    o, S = hgrn_scan(qf, key, v, logf, s0.astype(f32), chunk)
    o = o * lax.rsqrt(jnp.mean(o * o, axis=-1, keepdims=True) + EPS) * norm_g.astype(f32)
    o = o.reshape(B, T, B_WIDTH) * jax.nn.silu(g.astype(f32))
    return o.astype(q.dtype), S.astype(s0.dtype)


def moba_attend(q, k_full, v_full, q_pos):
    B, L, H, d = k_full.shape
    nb = -(-L // MOBA_BLOCK)
    padw = ((0, 0), (0, nb * MOBA_BLOCK - L), (0, 0), (0, 0))
    kb = jnp.pad(k_full, padw).reshape(B, nb, MOBA_BLOCK, H, d)
    vb = jnp.pad(v_full, padw).reshape(B, nb, MOBA_BLOCK, H, d)
    means = jnp.mean(kb.astype(jnp.float32), axis=2)
    q_blk = q_pos // MOBA_BLOCK
    gate = jnp.einsum('bthd,bnhd->bthn', q.astype(jnp.float32), means)
    past_ok = jnp.arange(nb)[None, :] < q_blk[:, None]
    gate = jnp.where(past_ok[None, :, None, :], gate, NEG)
    _, top = lax.top_k(gate, min(MOBA_TOP, nb))
    T = q.shape[1]
    own = jnp.broadcast_to(q_blk[None, :, None, None], (B, T, H, 1))
    idx = jnp.concatenate([top, own], axis=-1)
    ok = jnp.concatenate([top < q_blk[None, :, None, None], jnp.ones((B, T, H, 1), bool)], axis=-1)
    kbt = kb.transpose(0, 3, 1, 2, 4)
    vbt = vb.transpose(0, 3, 1, 2, 4)
    scale = d ** -0.5

    def attend(q_c, idx_c, ok_c, pos_c):
        bi = jnp.arange(q_c.shape[0])[:, None, None, None]
        hi = jnp.arange(H)[None, None, :, None]
        kg = kbt[bi, hi, idx_c]
        vg = vbt[bi, hi, idx_c]
        kpos = idx_c[..., None] * MOBA_BLOCK + jnp.arange(MOBA_BLOCK)
        mask = ok_c[..., None] & (kpos <= pos_c[:, :, None, None, None])
        sc = jnp.einsum('bchd,bchksd->bchks', q_c, kg) * scale
        sh = sc.shape
        p = masked_softmax(sc.reshape(sh[:3] + (-1,)), mask.reshape(mask.shape[:3] + (-1,))).reshape(sh)
        return jnp.einsum('bchks,bchksd->bchd', p.astype(vg.dtype), vg)

    return over_query_blocks(attend, MOBA_QBLOCK, q, idx, ok, q_pos[None])


def conv_ffn(h, prev, w_up, conv_w, conv_b, w_down):
    T = h.shape[1]
    u = h @ w_up
    full = jnp.concatenate([prev.astype(u.dtype), u], axis=1)
    y = conv_b + sum(conv_w[j] * full[:, j:j + T] for j in range(CONV_W))
    a, gte = jnp.split(y, 2, axis=-1)
    return (jax.nn.silu(a) * gte) @ w_down, full[:, -(CONV_W - 1):]


def token_mixers(h, pos0, past, w_in, w_out, cmp_pe, cmp_pool, cmp_w1, cmp_w2, lb, hgrn_g):
    B, T, _ = h.shape
    q_pos = pos0 + jnp.arange(T)
    cuts = np.cumsum(IN_SPLITS)[:-1].tolist()
    (a_q, a_kc, a_vc, a_ks, a_vs, a_kw, a_vw, a_gate,
     b_q, b_f, b_i, b_g, c_q, c_k, c_v) = jnp.split(h @ w_in, cuts, axis=-1)

    def heads(a, n):
        return a.reshape(B, T, n, HEAD_DIM)

    cmp_new = jnp.stack([heads(a_kc, A_KV_HEADS), heads(a_vc, A_KV_HEADS)], axis=2)
    sel_new = jnp.stack([rope(heads(a_ks, A_KV_HEADS), q_pos), heads(a_vs, A_KV_HEADS)], axis=2)
    win_new = jnp.stack([rope(heads(a_kw, A_KV_HEADS), q_pos), heads(a_vw, A_KV_HEADS)], axis=2)
    moba_new = jnp.stack([rope(heads(c_k, C_HEADS), q_pos), heads(c_v, C_HEADS)], axis=2)
    if past is None:
        cmp_all, sel_all, moba_all = cmp_new, sel_new, moba_new
        win_all, win_pos = win_new, q_pos
        s0 = jnp.zeros((B, B_HEADS, B_DK, B_DV), h.dtype)
    else:
        def cat(p, n):
            return jnp.concatenate([p.astype(n.dtype), n], axis=1)
        cmp_all = cat(past['cmp'], cmp_new)
        sel_all = cat(past['sel'], sel_new)
        moba_all = cat(past['moba'], moba_new)
        wb = past['win'].shape[1]
        win_all = cat(past['win'], win_new)
        win_pos = jnp.concatenate([pos0 - wb + jnp.arange(wb), q_pos])
        s0 = past['hgrn']
    win_state = win_all[:, -min(WINDOW, win_all.shape[1]):]
    banded = past is None and T % WIN_QBLOCK == 0
    o_a = nsa_mixer(a_q.reshape(B, T, A_KV_HEADS, A_REP, HEAD_DIM), cmp_all, sel_all, win_all, win_pos, q_pos,
                    a_gate, banded, cmp_pe, cmp_pool, cmp_w1, cmp_w2)
    o_b, s_new = hgrn_mixer(b_q, b_f, b_i, b_g, s0, lb, hgrn_g)
    o_c = moba_attend(rope(heads(c_q, C_HEADS), q_pos), moba_all[:, :, 0], moba_all[:, :, 1], q_pos).reshape(B, T, C_WIDTH)
    y = jnp.concatenate([o_a, o_b, o_c], axis=-1) @ w_out
    return y, (cmp_new, sel_new, win_state, s_new, moba_new)


def decoder_layer(x, c, pos0, past, w_ada, b_ada, n1, n2, w_in, w_out, cmp_pe, cmp_pool, cmp_w1, cmp_w2,
                  lb, hgrn_g, w_up, conv_w, conv_b, w_down):
    mod = (jax.nn.silu(c) @ w_ada + b_ada)[:, None, :]
    sh1, sc1, g1, sh2, sc2, g2 = jnp.split(mod, 6, axis=-1)
    h = rms_norm(x, n1) * (1 + sc1) + sh1
    mix, mstate = token_mixers(h, pos0, past, w_in, w_out, cmp_pe, cmp_pool, cmp_w1, cmp_w2, lb, hgrn_g)
    x = x + g1 * mix
    h = rms_norm(x, n2) * (1 + sc2) + sh2
    prev = jnp.zeros((x.shape[0], CONV_W - 1, 2 * D_FF), x.dtype) if past is None else past['conv']
    ff, conv_state = conv_ffn(h, prev, w_up, conv_w, conv_b, w_down)
    x = x + g2 * ff
    return x, mstate + (conv_state,)


def setup_inputs(seed: int = 0) -> dict:
    key = jax.random.key(seed)
    k = jax.random.split(key, 32)

    def nrm(i, shape, scale):
        return jax.random.normal(k[i], shape, jnp.float32) * scale

    n_pages = PAST_LEN // PAGE_SIZE
    n_pool = (DEC_BATCH * n_pages * 5) // 4
    win_buf = min(WINDOW, PAST_LEN)
    page_table = jax.random.permutation(k[0], n_pool)[:DEC_BATCH * n_pages].reshape(DEC_BATCH, n_pages).astype(jnp.int32)
    return {
        'x_prompt': nrm(1, (BATCH, SEQ, D_MODEL), 1.0),
        'x_sample': nrm(2, (DEC_BATCH, DEC_SEQ, D_MODEL), 1.0),
        'cache_nsa_cmp_kv': nrm(3, (DEPTH, n_pool, PAGE_SIZE, 2, A_KV_HEADS, HEAD_DIM), 1.0),
        'cache_nsa_sel_kv': nrm(4, (DEPTH, n_pool, PAGE_SIZE, 2, A_KV_HEADS, HEAD_DIM), 1.0),
        'cache_nsa_win_kv': nrm(5, (DEPTH, DEC_BATCH, win_buf, 2, A_KV_HEADS, HEAD_DIM), 1.0),
        'state_hgrn': nrm(6, (DEPTH, DEC_BATCH, B_HEADS, B_DK, B_DV), 0.5),
        'cache_moba_kv': nrm(7, (DEPTH, n_pool, PAGE_SIZE, 2, C_HEADS, HEAD_DIM), 1.0),
        'state_ffn_conv': nrm(8, (DEPTH, DEC_BATCH, CONV_W - 1, 2 * D_FF), 1.0),
        'page_table': page_table,
        'c_prompt': nrm(9, (BATCH, D_MODEL), 1.0),
        'c_sample': nrm(10, (DEC_BATCH, D_MODEL), 1.0),
        'w_ada': nrm(11, (DEPTH, D_MODEL, 6 * D_MODEL), 0.5 * D_MODEL ** -0.5),
        'b_ada': nrm(12, (DEPTH, 6 * D_MODEL), 0.02),
        'norm1_g': 1.0 + nrm(13, (DEPTH, D_MODEL), 0.02),
        'norm2_g': 1.0 + nrm(14, (DEPTH, D_MODEL), 0.02),
        'w_in': nrm(15, (DEPTH, D_MODEL, N_IN), D_MODEL ** -0.5),
        'w_out': nrm(16, (DEPTH, MIX_WIDTH, D_MODEL), MIX_WIDTH ** -0.5),
        'cmp_pe': nrm(17, (DEPTH, 2, CMP_LEN, HEAD_DIM), 0.5),
        'cmp_pool': nrm(18, (DEPTH, 2, CMP_LEN, HEAD_DIM), CMP_LEN ** -0.5),
        'cmp_w1': nrm(19, (DEPTH, 2, HEAD_DIM, CMP_HIDDEN), HEAD_DIM ** -0.5),
        'cmp_w2': nrm(20, (DEPTH, 2, CMP_HIDDEN, HEAD_DIM), CMP_HIDDEN ** -0.5),
        'hgrn_lb_logits': nrm(21, (DEPTH, B_HEADS * B_DK), 0.5),
        'hgrn_norm_g': 1.0 + nrm(22, (DEPTH, B_DV), 0.02),
        'w_up': nrm(23, (DEPTH, D_MODEL, 2 * D_FF), D_MODEL ** -0.5),
        'conv_w': nrm(24, (DEPTH, CONV_W, 2 * D_FF), CONV_W ** -0.5),
        'conv_b': nrm(25, (DEPTH, 2 * D_FF), 0.02),
        'w_down': nrm(26, (DEPTH, D_FF, D_MODEL), D_FF ** -0.5),
        'final_g': 1.0 + nrm(27, (D_MODEL,), 0.02),
    }


def reference(x_prompt, x_sample, cache_nsa_cmp_kv, cache_nsa_sel_kv, cache_nsa_win_kv, state_hgrn,
              cache_moba_kv, state_ffn_conv, page_table, c_prompt, c_sample, w_ada, b_ada, norm1_g, norm2_g,
              w_in, w_out, cmp_pe, cmp_pool, cmp_w1, cmp_w2, hgrn_lb_logits, hgrn_norm_g,
              w_up, conv_w, conv_b, w_down, final_g):
    past_len = page_table.shape[1] * cache_moba_kv.shape[2]
    lbs = hgrn_lower_bounds(hgrn_lb_logits)
    xp, xs = x_prompt, x_sample
    st_p, st_s = [], []
    for l in range(DEPTH):
        lp = (w_ada[l], b_ada[l], norm1_g[l], norm2_g[l], w_in[l], w_out[l], cmp_pe[l], cmp_pool[l],
              cmp_w1[l], cmp_w2[l], lbs[l], hgrn_norm_g[l], w_up[l], conv_w[l], conv_b[l], w_down[l])
        past = {'cmp': gather_pages(cache_nsa_cmp_kv[l], page_table),
                'sel': gather_pages(cache_nsa_sel_kv[l], page_table),
                'win': cache_nsa_win_kv[l],
                'hgrn': state_hgrn[l],
                'moba': gather_pages(cache_moba_kv[l], page_table),
                'conv': state_ffn_conv[l]}
        xp, new_p = decoder_layer(xp, c_prompt, 0, None, *lp)
        xs, new_s = decoder_layer(xs, c_sample, past_len, past, *lp)
        st_p.append(new_p)
        st_s.append(new_s)

    def stacked(sts, j):
        return jnp.stack([st[j] for st in sts], axis=0)

    y_prompt = rms_norm(xp, final_g)
    y_sample = rms_norm(xs, final_g)
    return (y_prompt, y_sample,
            stacked(st_p, 0), stacked(st_p, 1), stacked(st_p, 2), stacked(st_p, 3), stacked(st_p, 4), stacked(st_p, 5),
            stacked(st_s, 0), stacked(st_s, 1), stacked(st_s, 2), stacked(st_s, 3), stacked(st_s, 4), stacked(st_s, 5))
```

```python
import functools

import jax
import jax.numpy as jnp
from jax import lax
from jax.experimental import pallas as pl
from jax.experimental.pallas import tpu as pltpu

F32 = jnp.float32
BF16 = jnp.bfloat16

HEAD_DIM = 64
ROPE_THETA = 10000.0
EPS = 1e-6
NEG = -1e30
A_HEADS = 6
A_KV_HEADS = 2
A_REP = A_HEADS // A_KV_HEADS
CMP_LEN = 32
CMP_STRIDE = 16
CMP_HIDDEN = 128
SEL_BLOCK = 64
SEL_TOP = 16
FORCE_BONUS = 1e4
WINDOW = 512
B_HEADS = 4
B_DK = 64
B_DV = 64
C_HEADS = 6
MOBA_BLOCK = 256
MOBA_TOP = 3
CONV_W = 3

LANES = 128
HG_CHUNK = 16
Q_TILE = 128
VMEM_LIMIT = 56 * 1024 * 1024

A_WIDTH = A_HEADS * HEAD_DIM
KV_W = 2 * A_KV_HEADS * HEAD_DIM
B_W = B_HEADS * B_DK
C_WIDTH = C_HEADS * HEAD_DIM
GATE_PAD = LANES
OFF_QA = 0
OFF_CMP = OFF_QA + A_WIDTH
OFF_SEL = OFF_CMP + KV_W
OFF_WIN = OFF_SEL + KV_W
OFF_GATE = OFF_WIN + KV_W
OFF_HG = OFF_GATE + GATE_PAD
OFF_CQ = OFF_HG + 4 * B_W
OFF_CK = OFF_CQ + C_WIDTH
OFF_CV = OFF_CK + C_WIDTH
N_IN_PAD = OFF_CV + C_WIDTH


def _cp(*sem):
    return pltpu.CompilerParams(dimension_semantics=sem, vmem_limit_bytes=VMEM_LIMIT)


def _dot(a, b):
    return jnp.dot(a.astype(BF16), b.astype(BF16), preferred_element_type=F32)


def _dot_nt(a, b):
    return lax.dot_general(a.astype(BF16), b.astype(BF16), (((1,), (1,)), ((), ())),
                           preferred_element_type=F32)


def _dot_tn(a, b):
    return lax.dot_general(a.astype(BF16), b.astype(BF16), (((0,), (0,)), ((), ())),
                           preferred_element_type=F32)


def _split2(x):
    hi = x.astype(BF16)
    lo = (x - hi.astype(F32)).astype(BF16)
    return hi, lo


def _dot_x2(x, w_bf16):
    hi, lo = _split2(x)
    return (jnp.dot(hi, w_bf16, preferred_element_type=F32)
            + jnp.dot(lo, w_bf16, preferred_element_type=F32))


def _sigmoid(x):
    return 1.0 / (1.0 + jnp.exp(-x))


def _silu(x):
    return x * _sigmoid(x)


def _masked_softmax(s, mask):
    s = jnp.where(mask, s, NEG)
    m = jnp.max(s, axis=-1, keepdims=True)
    p = jnp.where(mask, jnp.exp(s - m), 0.0)
    return p / jnp.maximum(jnp.sum(p, axis=-1, keepdims=True), 1e-30)


def _iota(shape, dim):
    return lax.broadcasted_iota(jnp.int32, shape, dim)


def _rope128(x, cos, sin):
    lane = _iota(x.shape, 1)
    first = (lane & (HEAD_DIM - 1)) < (HEAD_DIM // 2)
    swapped = jnp.where(first, pltpu.roll(x, LANES - HEAD_DIM // 2, 1), pltpu.roll(x, HEAD_DIM // 2, 1))
    return x * cos + swapped * sin


def _rank_before(score, n):
    lane = _iota((1, score.shape[1]), 1)
    rank = jnp.zeros(score.shape, jnp.int32)
    for j in range(n):
        col = score[:, j:j + 1]
        beats = (col > score) | ((col == score) & (lane > j))
        rank = rank + beats.astype(jnp.int32)
    return rank


def _mod_kernel(c_ref, w_ref, b_ref, o_ref):
    o_ref[...] = _dot(_silu(c_ref[...]), w_ref[...]) + b_ref[...]


def _modulation(c_all, w_ada_bf, b_ada):
    depth, d, n6 = w_ada_bf.shape
    nb = c_all.shape[0]
    tn = n6 // 4
    return pl.pallas_call(
        _mod_kernel,
        grid=(depth, n6 // tn),
        in_specs=[pl.BlockSpec((nb, d), lambda l, j: (0, 0)),
                  pl.BlockSpec((None, d, tn), lambda l, j: (l, 0, j)),
                  pl.BlockSpec((None, 1, tn), lambda l, j: (l, 0, j))],
        out_specs=pl.BlockSpec((None, nb, tn), lambda l, j: (l, 0, j)),
        out_shape=jax.ShapeDtypeStruct((depth, nb, n6), F32),
        compiler_params=_cp("parallel", "parallel"),
        name="adaln_mod",
    )(c_all, w_ada_bf, b_ada.reshape(depth, 1, n6))


def _inproj_kernel(x_ref, sc_ref, sh_ref, g_ref, w_ref, cos_ref, sin_ref,
                   qa_ref, qar_ref, cmp_ref, sel_ref, win_ref, gate_ref, hg_ref, cq_ref, moba_ref):
    x = x_ref[...]
    h = x * lax.rsqrt(jnp.mean(x * x, axis=-1, keepdims=True) + EPS) * g_ref[...]
    h = (h * (1.0 + sc_ref[...]) + sh_ref[...]).astype(BF16)
    cos = cos_ref[...]
    sin = sin_ref[...]

    def proj(c0, n):
        return jnp.dot(h, w_ref[:, c0:c0 + n], preferred_element_type=F32)

    def rope_cols(y):
        return jnp.concatenate([_rope128(y[:, j:j + LANES], cos, sin) for j in range(0, y.shape[1], LANES)], axis=1)

    qa = proj(OFF_QA, A_WIDTH)
    qa_ref[...] = qa
    qar_ref[...] = rope_cols(qa)
    cmp_ref[...] = proj(OFF_CMP, KV_W)
    half = KV_W // 2
    s = proj(OFF_SEL, KV_W)
    sel_ref[:, :half] = rope_cols(s[:, :half])
    sel_ref[:, half:] = s[:, half:]
    w = proj(OFF_WIN, KV_W)
    win_ref[:, :half] = rope_cols(w[:, :half])
    win_ref[:, half:] = w[:, half:]
    gate_ref[...] = proj(OFF_GATE, GATE_PAD)
    hg_ref[...] = proj(OFF_HG, 4 * B_W)
    cq_ref[...] = rope_cols(proj(OFF_CQ, C_WIDTH))
    moba_ref[:, :C_WIDTH] = rope_cols(proj(OFF_CK, C_WIDTH))
    moba_ref[:, C_WIDTH:] = proj(OFF_CV, C_WIDTH)


def _in_projection(x2, sc, sh, gain, w_in_bf, cos, sin, *, tm, rows_per_mod, rows_per_pos):
    n, d = x2.shape
    pos_blocks = rows_per_pos // tm
    if rows_per_mod > 1:
        per = rows_per_mod // tm
        mod_spec = pl.BlockSpec((None, 1, d), lambda i: (i // per, 0, 0))
    else:
        mod_spec = pl.BlockSpec((tm, d), lambda i: (i, 0))
    widths = (A_WIDTH, A_WIDTH, KV_W, KV_W, KV_W, GATE_PAD, 4 * B_W, C_WIDTH, 2 * C_WIDTH)
    return pl.pallas_call(
        _inproj_kernel,
        grid=(n // tm,),
        in_specs=[pl.BlockSpec((tm, d), lambda i: (i, 0)), mod_spec, mod_spec,
                  pl.BlockSpec((1, d), lambda i: (0, 0)),
                  pl.BlockSpec((d, N_IN_PAD), lambda i: (0, 0)),
                  pl.BlockSpec((tm, LANES), lambda i: (i % pos_blocks, 0)),
                  pl.BlockSpec((tm, LANES), lambda i: (i % pos_blocks, 0))],
        out_specs=[pl.BlockSpec((tm, wd), lambda i: (i, 0)) for wd in widths],
        out_shape=[jax.ShapeDtypeStruct((n, wd), F32) for wd in widths],
        compiler_params=_cp("parallel"),
        name="in_projection",
    )(x2, sc, sh, gain, w_in_bf, cos, sin)


def _hgrn_kernel(hg_ref, st0_ref, lbl_ref, ng_ref, o_ref, st_out_ref,
                 st_scr, b_scr, key_scr, q_scr, v_scr, g_scr, o_scr, *, layer, valid, rows, nt):
    t = pl.program_id(1)

    @pl.when(t == 0)
    def _():
        st_scr[...] = st0_ref[...]

    lg = lbl_ref[...]
    e = jnp.exp(lg - jnp.max(lg, axis=0, keepdims=True))
    p = e / jnp.sum(e, axis=0, keepdims=True)
    acc = p[0:1, :]
    for i in range(1, layer + 1):
        acc = acc + p[i:i + 1, :]
    lb = jnp.clip(acc - p[0:1, :], 0.0, 1.0)
    lbpos = lb > 0.0

    if valid < rows:
        g_scr[...] = jnp.zeros_like(g_scr)
        v_scr[...] = jnp.zeros_like(v_scr)
        q_scr[...] = jnp.zeros_like(q_scr)
        key_scr[...] = jnp.zeros_like(key_scr)
        q_scr[0:valid, :] = hg_ref[:, 0:B_W]
        key_scr[0:valid, :] = hg_ref[:, B_W:2 * B_W]
        v_scr[0:valid, :] = hg_ref[:, 2 * B_W:3 * B_W]
        g_scr[0:valid, :] = hg_ref[:, 3 * B_W:4 * B_W]
        q = q_scr[...]
        z = key_scr[...]
    else:
        q = hg_ref[:, 0:B_W]
        z = hg_ref[:, B_W:2 * B_W]
        v_scr[...] = hg_ref[:, 2 * B_W:3 * B_W]
        g_scr[...] = hg_ref[:, 3 * B_W:4 * B_W]

    ls = jnp.minimum(z, 0.0) - jnp.log1p(jnp.exp(-jnp.abs(z)))
    c_ = jnp.log1p(-lb) + ls
    a_ = jnp.log(jnp.where(lbpos, lb, 1.0))
    logf = jnp.where(lbpos, jnp.maximum(a_, c_) + jnp.log1p(jnp.exp(-jnp.abs(a_ - c_))), c_)
    key = (1.0 - lb) * _sigmoid(-z)
    rowi = _iota((rows, 1), 0)
    live = rowi < valid
    logf = jnp.where(live, logf, 0.0)
    key = jnp.where(live, key, 0.0)
    ri = _iota((rows, rows), 0)
    ci = _iota((rows, rows), 1)
    tri = jnp.where(((ri // HG_CHUNK) == (ci // HG_CHUNK)) & (ci <= ri), 1.0, 0.0).astype(BF16)
    h1 = logf.astype(BF16)
    r1 = logf - h1.astype(F32)
    h2 = r1.astype(BF16)
    h3 = (r1 - h2.astype(F32)).astype(BF16)
    b_scr[...] = (jnp.dot(tri, h1, preferred_element_type=F32) + jnp.dot(tri, h2, preferred_element_type=F32)
                  + jnp.dot(tri, h3, preferred_element_type=F32))
    key_scr[...] = key
    q_scr[...] = _silu(q)

    hr = _iota((B_W, B_W), 0)
    hc = _iota((B_W, B_W), 1)
    same_head = (hr // B_DK) == (hc // B_DK)
    head_ones = jnp.where(same_head, 1.0, 0.0).astype(BF16)
    ti = _iota((HG_CHUNK, 1), 0)
    ng = ng_ref[...]

    def chunk(c, carry):
        r0 = pl.multiple_of(c * HG_CHUNK, HG_CHUNK)
        b = b_scr[pl.ds(r0, HG_CHUNK), :]
        kk = key_scr[pl.ds(r0, HG_CHUNK), :]
        qf = q_scr[pl.ds(r0, HG_CHUNK), :]
        v = v_scr[pl.ds(r0, HG_CHUNK), :]
        gt = g_scr[pl.ds(r0, HG_CHUNK), :]
        st = st_scr[...]
        blast = b[HG_CHUNK - 1:HG_CHUNK, :]
        o_inter = _dot_nt(qf * jnp.exp(b), st)
        es = []
        for s in range(HG_CHUNK):
            e_s = qf * kk[s:s + 1, :] * jnp.exp(jnp.minimum(b - b[s:s + 1, :], 0.0))
            es.append(jnp.where(ti >= s, e_s, 0.0))
        r = _dot_x2(jnp.concatenate(es, axis=0), head_ones)
        o = o_inter
        for s in range(HG_CHUNK):
            o = o + r[s * HG_CHUNK:(s + 1) * HG_CHUNK, :] * v[s:s + 1, :]
        ms = _dot_x2(o * o, head_ones) * (1.0 / B_DV)
        o_scr[pl.ds(r0, HG_CHUNK), :] = o * lax.rsqrt(ms + EPS) * ng * _silu(gt)
        upd = _dot_tn(v, kk * jnp.exp(blast - b))
        st_scr[...] = st * jnp.exp(blast) + jnp.where(same_head, upd, 0.0)
        return carry

    lax.fori_loop(0, rows // HG_CHUNK, chunk, 0)
    o_ref[...] = o_scr[0:valid, :]

    @pl.when(t == nt - 1)
    def _():
        st_out_ref[...] = st_scr[...]


def _hgrn(hg, st0, lb_logits, norm_g4, *, layer, batch, seq):
    if seq % HG_CHUNK == 0:
        tt = min(seq, 256)
        assert seq % tt == 0
        nt = seq // tt
        rows = valid = tt
        hg_in = hg
        hg_spec = pl.BlockSpec((tt, 4 * B_W), lambda b, t: (b * nt + t, 0))
        o_spec = pl.BlockSpec((tt, B_W), lambda b, t: (b * nt + t, 0))
        o_shape = jax.ShapeDtypeStruct((batch * seq, B_W), F32)
    else:
        assert seq < HG_CHUNK
        nt, rows, valid = 1, HG_CHUNK, seq
        hg_in = hg.reshape(batch, seq, 4 * B_W)
        hg_spec = pl.BlockSpec((None, seq, 4 * B_W), lambda b, t: (b, 0, 0))
        o_spec = pl.BlockSpec((None, seq, B_W), lambda b, t: (b, 0, 0))
        o_shape = jax.ShapeDtypeStruct((batch, seq, B_W), F32)
    depth = lb_logits.shape[0]
    kern = functools.partial(_hgrn_kernel, layer=layer, valid=valid, rows=rows, nt=nt)
    o, st = pl.pallas_call(
        kern,
        grid=(batch, nt),
        in_specs=[hg_spec,
                  pl.BlockSpec((None, B_W, B_W), lambda b, t: (b, 0, 0)),
                  pl.BlockSpec((depth, B_W), lambda b, t: (0, 0)),
                  pl.BlockSpec((1, B_W), lambda b, t: (0, 0))],
        out_specs=[o_spec, pl.BlockSpec((None, B_W, B_W), lambda b, t: (b, 0, 0))],
        out_shape=[o_shape, jax.ShapeDtypeStruct((batch, B_W, B_W), F32)],
        scratch_shapes=[pltpu.VMEM((B_W, B_W), F32)] + [pltpu.VMEM((rows, B_W), F32)] * 6,
        compiler_params=_cp("parallel", "arbitrary"),
        name="hgrn2",
    )(hg_in, st0, lb_logits, norm_g4)
    return o.reshape(batch * seq, B_W), st


def _state_to_blockdiag(s0):
    b = s0.shape[0]
    eye = jnp.eye(B_HEADS, dtype=s0.dtype)
    return jnp.einsum('bhkv,hg->bhvgk', s0, eye).reshape(b, B_W, B_W)


def _blockdiag_to_state(st):
    b = st.shape[0]
    eye = jnp.eye(B_HEADS, dtype=st.dtype)
    return jnp.einsum('bhvgk,hg->bhkv', st.reshape(b, B_HEADS, B_DV, B_HEADS, B_DK), eye)


def _nsa_compress(rows_ref, nblk, pe_ref, pool_ref, w1_ref, w2_ref):
    pool = pool_ref[...]
    x3 = rows_ref[0:nblk * CMP_STRIDE, :].reshape(nblk, CMP_STRIDE, KV_W)
    acc_a = jnp.sum(x3 * pool[0:CMP_STRIDE, :][None], axis=1)
    acc_b = jnp.sum(x3 * pool[CMP_STRIDE:2 * CMP_STRIDE, :][None], axis=1)
    acc_b = jnp.concatenate([acc_b[1:, :], acc_b[:1, :]], axis=0)
    bias = jnp.sum(pe_ref[...] * pool, axis=0, keepdims=True)
    pooled = acc_a + acc_b + bias
    hid = _silu(_dot(pooled, w1_ref[...]))
    return _dot(hid, w2_ref[...])


def _nsa_core(qa, qar, gate, cmpkv, sel_ref, win_ref, qpos, *, n_cmp, n_sel, sel_len,
              win_start, win_span, win_kbase):
    tq = qa.shape[0]
    ncp = cmpkv.shape[0]
    scale = HEAD_DIM ** -0.5
    gs = _sigmoid(gate)
    n_i = _iota((1, ncp), 1)
    cmp_mask = (n_i * CMP_STRIDE + (CMP_LEN - 1) <= qpos) & (n_i < n_cmp)
    on = _iota((ncp, LANES), 0) * CMP_STRIDE
    os_ = _iota((ncp, LANES), 1) * SEL_BLOCK
    overlap = jnp.where((on < os_ + SEL_BLOCK) & (on + CMP_LEN > os_), 1.0, 0.0).astype(BF16)
    s_i = _iota((1, LANES), 1)
    cur = qpos // SEL_BLOCK
    forced = (s_i == 0) | (s_i == cur) | (s_i == cur - 1)
    visible = (s_i * SEL_BLOCK <= qpos) & (s_i < n_sel)
    es = _iota((LANES, sel_len), 0)
    ej = _iota((LANES, sel_len), 1)
    expand = jnp.where((ej // SEL_BLOCK) == es, 1.0, 0.0).astype(BF16)
    kpos_sel = _iota((1, sel_len), 1)
    causal_sel = kpos_sel <= qpos
    kpos_win = win_kbase + _iota((1, win_span), 1)
    win_mask = (kpos_win <= qpos) & (qpos - kpos_win <= WINDOW)
    half = KV_W // 2
    outs = []
    for g in range(A_KV_HEADS):
        kc = cmpkv[:, g * HEAD_DIM:(g + 1) * HEAD_DIM]
        vc = cmpkv[:, half + g * HEAD_DIM:half + (g + 1) * HEAD_DIM]
        o_cmp = []
        p_sum = jnp.zeros((tq, ncp), F32)
        for r in range(A_REP):
            h = g * A_REP + r
            p = _masked_softmax(_dot_nt(qa[:, h * HEAD_DIM:(h + 1) * HEAD_DIM], kc) * scale, cmp_mask)
            o_cmp.append(_dot(p, vc))
            p_sum = p_sum + p
        imp = _dot_x2(p_sum, overlap)
        score = jnp.where(visible, imp + jnp.where(forced, FORCE_BONUS, 0.0), NEG)
        chosen = _rank_before(score, n_sel) < min(SEL_TOP, n_sel)
        chosen_keys = jnp.dot(jnp.where(chosen, 1.0, 0.0).astype(BF16), expand, preferred_element_type=F32)
        sel_mask = (chosen_keys > 0.5) & causal_sel
        ks = sel_ref[:, g * HEAD_DIM:(g + 1) * HEAD_DIM]
        vs = sel_ref[:, half + g * HEAD_DIM:half + (g + 1) * HEAD_DIM]
        kw = win_ref[pl.ds(win_start, win_span), g * HEAD_DIM:(g + 1) * HEAD_DIM]
        vw = win_ref[pl.ds(win_start, win_span), half + g * HEAD_DIM:half + (g + 1) * HEAD_DIM]
        for r in range(A_REP):
            h = g * A_REP + r
            q = qar[:, h * HEAD_DIM:(h + 1) * HEAD_DIM]
            o_sel = _dot(_masked_softmax(_dot_nt(q, ks) * scale, sel_mask), vs)
            o_win = _dot(_masked_softmax(_dot_nt(q, kw) * scale, win_mask), vw)
            c0 = 3 * h
            outs.append(gs[:, c0:c0 + 1] * o_cmp[r] + gs[:, c0 + 1:c0 + 2] * o_sel + gs[:, c0 + 2:c0 + 3] * o_win)
    return jnp.concatenate(outs, axis=1)


def _compress_kernel(rows_ref, pe_ref, pool_ref, w1_ref, w2_ref, o_ref, *, nblk):
    o_ref[...] = _nsa_compress(rows_ref, nblk, pe_ref, pool_ref, w1_ref, w2_ref)


def _nsa_prompt_kernel(qa_ref, qar_ref, gate_ref, cmp_ref, sel_ref, win_ref, o_ref, *, seq, tq, n_cmp, n_sel, span):
    q0 = pl.program_id(1) * tq
    qpos = q0 + _iota((tq, 1), 0)
    start = pl.multiple_of(jnp.clip(q0 - WINDOW, 0, seq - span), tq)
    o_ref[...] = _nsa_core(qa_ref[...], qar_ref[...], gate_ref[...], cmp_ref[...], sel_ref, win_ref, qpos,
                           n_cmp=n_cmp, n_sel=n_sel, sel_len=seq, win_start=start, win_span=span, win_kbase=start)


def _nsa_prompt(qa, qar, gate, cmp_rows, sel_rows, win_rows, cw, *, batch, seq):
    pe4, pool4, w1bd, w2bd = cw
    assert seq % Q_TILE == 0 and seq % CMP_STRIDE == 0 and seq >= CMP_LEN
    nblk = seq // CMP_STRIDE
    n_cmp = (seq - CMP_LEN) // CMP_STRIDE + 1
    n_sel = -(-seq // SEL_BLOCK)
    assert nblk <= LANES and n_sel <= LANES and nblk % 8 == 0
    tq = Q_TILE
    nq = seq // tq
    span = min(seq, tq + WINDOW)
    cmpkv = pl.pallas_call(
        functools.partial(_compress_kernel, nblk=nblk),
        grid=(batch,),
        in_specs=[pl.BlockSpec((seq, KV_W), lambda b: (b, 0)),
                  pl.BlockSpec(pe4.shape, lambda b: (0, 0)),
                  pl.BlockSpec(pool4.shape, lambda b: (0, 0)),
                  pl.BlockSpec(w1bd.shape, lambda b: (0, 0)),
                  pl.BlockSpec(w2bd.shape, lambda b: (0, 0))],
        out_specs=pl.BlockSpec((None, nblk, KV_W), lambda b: (b, 0, 0)),
        out_shape=jax.ShapeDtypeStruct((batch, nblk, KV_W), F32),
        compiler_params=_cp("parallel"),
        name="nsa_compress",
    )(cmp_rows, pe4, pool4, w1bd, w2bd)
    kern = functools.partial(_nsa_prompt_kernel, seq=seq, tq=tq, n_cmp=n_cmp, n_sel=n_sel, span=span)
    return pl.pallas_call(
        kern,
        grid=(batch, nq),
        in_specs=[pl.BlockSpec((tq, A_WIDTH), lambda b, i: (b * nq + i, 0)),
                  pl.BlockSpec((tq, A_WIDTH), lambda b, i: (b * nq + i, 0)),
                  pl.BlockSpec((tq, GATE_PAD), lambda b, i: (b * nq + i, 0)),
                  pl.BlockSpec((None, nblk, KV_W), lambda b, i: (b, 0, 0)),
                  pl.BlockSpec((seq, KV_W), lambda b, i: (b, 0)),
                  pl.BlockSpec((seq, KV_W), lambda b, i: (b, 0))],
        out_specs=pl.BlockSpec((tq, A_WIDTH), lambda b, i: (b * nq + i, 0)),
        out_shape=jax.ShapeDtypeStruct((batch * seq, A_WIDTH), F32),
        compiler_params=_cp("parallel", "parallel"),
        name="nsa_prompt",
    )(qa, qar, gate, cmpkv, sel_rows, win_rows)


def _nsa_sample_kernel(pt_ref, *refs, n_pages, page, seq, past, wb, w_out):
    cmp_pages = refs[:n_pages]
    sel_pages = refs[n_pages:2 * n_pages]
    (wcache_ref, qa_ref, qar_ref, gate_ref, seln_ref, winn_ref, pe_ref, pool_ref, w1_ref, w2_ref,
     o_ref, wstate_ref, cmp_scr, sel_scr, win_scr, q_scr) = refs[2 * n_pages:]
    del pt_ref
    for j in range(n_pages):
        cmp_scr[j * page:(j + 1) * page, :] = cmp_pages[j][...]
        sel_scr[j * page:(j + 1) * page, :] = sel_pages[j][...]
    sel_scr[past:, :] = jnp.zeros((sel_scr.shape[0] - past, KV_W), F32)
    sel_scr[past:past + seq, :] = seln_ref[...]
    win_scr[0:wb, :] = wcache_ref[...]
    win_scr[wb:, :] = jnp.zeros((win_scr.shape[0] - wb, KV_W), F32)
    win_scr[wb:wb + seq, :] = winn_ref[...]
    wstate_ref[...] = win_scr[wb + seq - w_out:wb + seq, :]
    tq = q_scr.shape[1]
    q_scr[...] = jnp.zeros_like(q_scr)
    q_scr[0, 0:seq, :] = qa_ref[...]
    q_scr[1, 0:seq, :] = qar_ref[...]
    q_scr[2, 0:seq, 0:GATE_PAD] = gate_ref[...]
    nblk = past // CMP_STRIDE
    total = past + seq
    n_cmp = (total - CMP_LEN) // CMP_STRIDE + 1
    n_sel = -(-total // SEL_BLOCK)
    cmpkv = _nsa_compress(cmp_scr, nblk, pe_ref, pool_ref, w1_ref, w2_ref)
    qpos = past + _iota((tq, 1), 0)
    o = _nsa_core(q_scr[0], q_scr[1], q_scr[2][:, 0:GATE_PAD], cmpkv, sel_scr, win_scr, qpos,
                  n_cmp=n_cmp, n_sel=n_sel, sel_len=sel_scr.shape[0],
                  win_start=0, win_span=win_scr.shape[0], win_kbase=past - wb)
    o_ref[...] = o[0:seq, :]


def _nsa_sample(qa, qar, gate, sel_new, win_new, cmp_pool_pages, sel_pool_pages, win_cache, page_table, cw,
                *, layer, batch, seq):
    pe4, pool4, w1bd, w2bd = cw
    n_pages = page_table.shape[1]
    page = cmp_pool_pages.shape[1]
    n_pool = cmp_pool_pages.shape[0] // (win_cache.shape[0] // batch)
    past = n_pages * page
    wb = win_cache.shape[1]
    w_out = min(WINDOW, wb + seq)
    total = past + seq
    nblk = past // CMP_STRIDE
    assert past % CMP_STRIDE == 0 and seq < CMP_STRIDE and past >= CMP_LEN and seq <= 8
    assert (total - CMP_LEN) // CMP_STRIDE + 1 == nblk - 1 and nblk % 8 == 0 and nblk <= LANES
    assert -(-total // SEL_BLOCK) <= LANES and past % LANES == 0 and wb % 8 == 0
    base = layer * n_pool

    def page_spec(j):
        return pl.BlockSpec((None, page, KV_W), lambda b, pt: (base + pt[b, j], 0, 0))

    def tok_spec(wd):
        return pl.BlockSpec((None, seq, wd), lambda b, pt: (b, 0, 0))

    def full_spec(a):
        return pl.BlockSpec(a.shape, lambda b, pt: (0,) * a.ndim)

    kern = functools.partial(_nsa_sample_kernel, n_pages=n_pages, page=page, seq=seq, past=past, wb=wb, w_out=w_out)
    r3 = lambda a: a.reshape(batch, seq, a.shape[-1])
    o, wstate = pl.pallas_call(
        kern,
        grid_spec=pltpu.PrefetchScalarGridSpec(
            num_scalar_prefetch=1,
            grid=(batch,),
            in_specs=[page_spec(j) for j in range(n_pages)] + [page_spec(j) for j in range(n_pages)]
            + [pl.BlockSpec((None, wb, KV_W), lambda b, pt: (layer * batch + b, 0, 0)),
               tok_spec(A_WIDTH), tok_spec(A_WIDTH), tok_spec(GATE_PAD), tok_spec(KV_W), tok_spec(KV_W),
               full_spec(pe4), full_spec(pool4), full_spec(w1bd), full_spec(w2bd)],
            out_specs=[tok_spec(A_WIDTH), pl.BlockSpec((None, w_out, KV_W), lambda b, pt: (b, 0, 0))],
            scratch_shapes=[pltpu.VMEM((past, KV_W), F32),
                            pltpu.VMEM((past + LANES, KV_W), F32),
                            pltpu.VMEM((wb + LANES, KV_W), F32),
                            pltpu.VMEM((3, 8, A_WIDTH), F32)]),
        out_shape=[jax.ShapeDtypeStruct((batch, seq, A_WIDTH), F32),
                   jax.ShapeDtypeStruct((batch, w_out, KV_W), F32)],
        compiler_params=_cp("parallel"),
        name="nsa_sample",
    )(page_table, *([cmp_pool_pages] * n_pages), *([sel_pool_pages] * n_pages), win_cache,
      r3(qa), r3(qar), r3(gate), r3(sel_new), r3(win_new), pe4, pool4, w1bd, w2bd)
    return o.reshape(batch * seq, A_WIDTH), wstate


def _moba_means(kv_ref, means_scr, n_blocks, kv_len):
    means_scr[...] = jnp.zeros_like(means_scr)
    for n in range(n_blocks):
        lo = n * MOBA_BLOCK
        hi = min(lo + MOBA_BLOCK, kv_len)
        means_scr[n:n + 1, :] = jnp.sum(kv_ref[lo:hi, 0:C_WIDTH], axis=0, keepdims=True) * (1.0 / MOBA_BLOCK)


def _moba_core(q, kv_ref, means, qpos, *, n_blocks, kv_len):
    scale = HEAD_DIM ** -0.5
    n_i = _iota((1, LANES), 1)
    q_blk = qpos // MOBA_BLOCK
    past_ok = n_i < q_blk
    es = _iota((LANES, kv_len), 0)
    ej = _iota((LANES, kv_len), 1)
    expand = jnp.where((ej // MOBA_BLOCK) == es, 1.0, 0.0).astype(BF16)
    kpos = _iota((1, kv_len), 1)
    own = ((kpos // MOBA_BLOCK) == q_blk) & (kpos <= qpos)
    outs = []
    for h in range(C_HEADS):
        qh = q[:, h * HEAD_DIM:(h + 1) * HEAD_DIM]
        mh = means[:, h * HEAD_DIM:(h + 1) * HEAD_DIM]
        q_hi, q_lo = _split2(qh)
        m_hi, m_lo = _split2(mh)
        nt = (((1,), (1,)), ((), ()))
        gate = (lax.dot_general(q_hi, m_hi, nt, preferred_element_type=F32)
                + lax.dot_general(q_lo, m_hi, nt, preferred_element_type=F32)
                + lax.dot_general(q_hi, m_lo, nt, preferred_element_type=F32))
        gate = jnp.where(past_ok, gate, NEG)
        chosen = past_ok & (_rank_before(gate, n_blocks) < min(MOBA_TOP, n_blocks))
        chosen_keys = jnp.dot(jnp.where(chosen, 1.0, 0.0).astype(BF16), expand, preferred_element_type=F32)
        mask = (chosen_keys > 0.5) | own
        kh = kv_ref[:, h * HEAD_DIM:(h + 1) * HEAD_DIM]
        vh = kv_ref[:, C_WIDTH + h * HEAD_DIM:C_WIDTH + (h + 1) * HEAD_DIM]
        outs.append(_dot(_masked_softmax(_dot_nt(qh, kh) * scale, mask), vh))
    return jnp.concatenate(outs, axis=1)


def _moba_prompt_kernel(q_ref, kv_ref, o_ref, means_scr, *, seq, tq, n_blocks):
    @pl.when(pl.program_id(1) == 0)
    def _():
        _moba_means(kv_ref, means_scr, n_blocks, seq)

    qpos = pl.program_id(1) * tq + _iota((tq, 1), 0)
    o_ref[...] = _moba_core(q_ref[...], kv_ref, means_scr[...], qpos, n_blocks=n_blocks, kv_len=seq)


def _moba_prompt(cq, kv_rows, *, batch, seq):
    tq = Q_TILE
    nq = seq // tq
    n_blocks = -(-seq // MOBA_BLOCK)
    assert seq % tq == 0 and n_blocks <= LANES
    kern = functools.partial(_moba_prompt_kernel, seq=seq, tq=tq, n_blocks=n_blocks)
    return pl.pallas_call(
        kern,
        grid=(batch, nq),
        in_specs=[pl.BlockSpec((tq, C_WIDTH), lambda b, i: (b * nq + i, 0)),
                  pl.BlockSpec((seq, 2 * C_WIDTH), lambda b, i: (b, 0))],
        out_specs=pl.BlockSpec((tq, C_WIDTH), lambda b, i: (b * nq + i, 0)),
        out_shape=jax.ShapeDtypeStruct((batch * seq, C_WIDTH), F32),
        scratch_shapes=[pltpu.VMEM((LANES, C_WIDTH), F32)],
        compiler_params=_cp("parallel", "arbitrary"),
        name="moba_prompt",
    )(cq, kv_rows)


def _moba_sample_kernel(pt_ref, *refs, n_pages, page, seq, past):
    pages = refs[:n_pages]
    q_ref, new_ref, o_ref, kv_scr, means_scr, q_scr = refs[n_pages:]
    del pt_ref
    for j in range(n_pages):
        kv_scr[j * page:(j + 1) * page, :] = pages[j][...]
    kv_scr[past:, :] = jnp.zeros((kv_scr.shape[0] - past, 2 * C_WIDTH), F32)
    kv_scr[past:past + seq, :] = new_ref[...]
    tq = q_scr.shape[0]
    q_scr[...] = jnp.zeros_like(q_scr)
    q_scr[0:seq, :] = q_ref[...]
    kv_len = kv_scr.shape[0]
    n_blocks = -(-(past + seq) // MOBA_BLOCK)
    _moba_means(kv_scr, means_scr, n_blocks, kv_len)
    qpos = past + _iota((tq, 1), 0)
    o = _moba_core(q_scr[...], kv_scr, means_scr[...], qpos, n_blocks=n_blocks, kv_len=kv_len)
    o_ref[...] = o[0:seq, :]


def _moba_sample(cq, kv_new, pool_pages, page_table, *, layer, depth, batch, seq):
    n_pages = page_table.shape[1]
    page = pool_pages.shape[1]
    n_pool = pool_pages.shape[0] // depth
    past = n_pages * page
    assert past % LANES == 0 and seq <= 8 and -(-(past + seq) // MOBA_BLOCK) <= LANES
    base = layer * n_pool
    kern = functools.partial(_moba_sample_kernel, n_pages=n_pages, page=page, seq=seq, past=past)
    r3 = lambda a: a.reshape(batch, seq, a.shape[-1])
    o = pl.pallas_call(
        kern,
        grid_spec=pltpu.PrefetchScalarGridSpec(
            num_scalar_prefetch=1,
            grid=(batch,),
            in_specs=[pl.BlockSpec((None, page, 2 * C_WIDTH), (lambda b, pt, j=j: (base + pt[b, j], 0, 0)))
                      for j in range(n_pages)]
            + [pl.BlockSpec((None, seq, C_WIDTH), lambda b, pt: (b, 0, 0)),
               pl.BlockSpec((None, seq, 2 * C_WIDTH), lambda b, pt: (b, 0, 0))],
            out_specs=pl.BlockSpec((None, seq, C_WIDTH), lambda b, pt: (b, 0, 0)),
            scratch_shapes=[pltpu.VMEM((past + LANES, 2 * C_WIDTH), F32),
                            pltpu.VMEM((LANES, C_WIDTH), F32),
                            pltpu.VMEM((8, C_WIDTH), F32)]),
        out_shape=jax.ShapeDtypeStruct((batch, seq, C_WIDTH), F32),
        compiler_params=_cp("parallel"),
        name="moba_sample",
    )(page_table, *([pool_pages] * n_pages), r3(cq), r3(kv_new))
    return o.reshape(batch * seq, C_WIDTH)


FF_CHUNK = 256


def _ffn_kernel(*refs, seq, tm, d_ff, paged_prev, tiles_per_seq):
    if paged_prev:
        (x_ref, oa_ref, ob_ref, oc_ref, g1_ref, sc2_ref, sh2_ref, g2_ref, n2_ref, wout_ref, wup_ref,
         cw_ref, cb_ref, wdn_ref, p0_ref, p1_ref, y_ref, u_ref, ubuf, acc_scr) = refs
    else:
        (x_ref, oa_ref, ob_ref, oc_ref, g1_ref, sc2_ref, sh2_ref, g2_ref, n2_ref, wout_ref, wup_ref,
         cw_ref, cb_ref, wdn_ref, y_ref, cst_ref, ubuf, acc_scr, carry) = refs
    i = pl.program_id(0)
    mix = (_dot(oa_ref[...], wout_ref[0:A_WIDTH, :])
           + _dot(ob_ref[...], wout_ref[A_WIDTH:A_WIDTH + B_W, :])
           + _dot(oc_ref[...], wout_ref[A_WIDTH + B_W:, :]))
    x1 = x_ref[...] + g1_ref[...] * mix
    h = x1 * lax.rsqrt(jnp.mean(x1 * x1, axis=-1, keepdims=True) + EPS) * n2_ref[...]
    h = (h * (1.0 + sc2_ref[...]) + sh2_ref[...]).astype(BF16)
    acc_scr[...] = jnp.zeros_like(acc_scr)
    if paged_prev:
        trow = _iota((tm, 1), 0) % seq
    else:
        @pl.when(i % tiles_per_seq == 0)
        def _():
            carry[...] = jnp.zeros_like(carry)
    tf = FF_CHUNK
    hdr = 8

    def conv_half(c0):
        c0 = pl.multiple_of(c0, tf)
        u = jnp.dot(h, wup_ref[:, pl.ds(c0, tf)], preferred_element_type=F32)
        ubuf[hdr:hdr + tm, :] = u
        if paged_prev:
            u_ref[:, pl.ds(c0, tf)] = u
            p0 = p0_ref[:, pl.ds(c0, tf)]
            p1 = p1_ref[:, pl.ds(c0, tf)]
            u1 = jnp.where(trow == 0, p1, ubuf[hdr - 1:hdr - 1 + tm, :])
            u2 = jnp.where(trow == 0, p0, jnp.where(trow == 1, p1, ubuf[hdr - 2:hdr - 2 + tm, :]))
        else:
            ubuf[hdr - 2:hdr, :] = carry[:, pl.ds(c0, tf)]
            u1 = ubuf[hdr - 1:hdr - 1 + tm, :]
            u2 = ubuf[hdr - 2:hdr - 2 + tm, :]
            carry[:, pl.ds(c0, tf)] = u[tm - 2:tm, :]
            cst_ref[:, pl.ds(c0, tf)] = u[tm - 2:tm, :]
        w = cw_ref[:, pl.ds(c0, tf)]
        return cb_ref[:, pl.ds(c0, tf)] + w[0:1, :] * u2 + w[1:2, :] * u1 + w[2:3, :] * u

    def chunk(c, carry_):
        f0 = c * tf
        a = conv_half(f0)
        gte = conv_half(d_ff + f0)
        act = (_silu(a) * gte).astype(BF16)
        acc_scr[...] += jnp.dot(act, wdn_ref[pl.ds(pl.multiple_of(f0, tf), tf), :], preferred_element_type=F32)
        return carry_

    lax.fori_loop(0, d_ff // tf, chunk, 0)
    y_ref[...] = x1 + g2_ref[...] * acc_scr[...]


def _ffn(x2, oa, ob, oc, mods, n2, wout_bf, wup_bf, conv_w, conv_b, wdn_bf, prev, *, tm, seq, rows_per_mod):
    n, d = x2.shape
    d_ff = wdn_bf.shape[0]
    assert d_ff % FF_CHUNK == 0 and n % tm == 0
    if rows_per_mod > 1:
        per = rows_per_mod // tm
        mod_spec = pl.BlockSpec((None, 1, d), lambda i: (i // per, 0, 0))
    else:
        mod_spec = pl.BlockSpec((tm, d), lambda i: (i, 0))
    row = lambda wd: pl.BlockSpec((tm, wd), lambda i: (i, 0))
    full = lambda a: pl.BlockSpec(a.shape, lambda i: (0,) * a.ndim)
    in_specs = [row(d), row(A_WIDTH), row(B_W), row(C_WIDTH), mod_spec, mod_spec, mod_spec, mod_spec,
                full(n2), full(wout_bf), full(wup_bf), full(conv_w), full(conv_b), full(wdn_bf)]
    args = [x2, oa, ob, oc, *mods, n2, wout_bf, wup_bf, conv_w, conv_b, wdn_bf]
    scratch = [pltpu.VMEM((tm + 8, FF_CHUNK), F32), pltpu.VMEM((tm, d), F32)]
    if prev is None:
        assert seq % tm == 0 and tm >= 2
        tiles_per_seq = seq // tm
        out_specs = [row(d), pl.BlockSpec((None, CONV_W - 1, 2 * d_ff), lambda i: (i // tiles_per_seq, 0, 0))]
        out_shape = [jax.ShapeDtypeStruct((n, d), F32), jax.ShapeDtypeStruct((n // seq, CONV_W - 1, 2 * d_ff), F32)]
        scratch.append(pltpu.VMEM((CONV_W - 1, 2 * d_ff), F32))
        sem = "arbitrary"
    else:
        assert tm % seq == 0 and seq >= 2
        tiles_per_seq = 1
        in_specs += [row(2 * d_ff), row(2 * d_ff)]
        args += list(prev)
        out_specs = [row(d), row(2 * d_ff)]
        out_shape = [jax.ShapeDtypeStruct((n, d), F32), jax.ShapeDtypeStruct((n, 2 * d_ff), F32)]
        sem = "parallel"
    kern = functools.partial(_ffn_kernel, seq=seq, tm=tm, d_ff=d_ff, paged_prev=prev is not None,
                             tiles_per_seq=tiles_per_seq)
    return pl.pallas_call(
        kern, grid=(n // tm,), in_specs=in_specs, out_specs=out_specs, out_shape=out_shape,
        scratch_shapes=scratch, compiler_params=_cp(sem), name="outproj_convffn",
    )(*args)


def _final_norm_kernel(x_ref, g_ref, o_ref):
    x = x_ref[...]
    o_ref[...] = x * lax.rsqrt(jnp.mean(x * x, axis=-1, keepdims=True) + EPS) * g_ref[...]


def _final_norm(x2, g, tm):
    n, d = x2.shape
    return pl.pallas_call(
        _final_norm_kernel, grid=(n // tm,),
        in_specs=[pl.BlockSpec((tm, d), lambda i: (i, 0)), pl.BlockSpec((1, d), lambda i: (0, 0))],
        out_specs=pl.BlockSpec((tm, d), lambda i: (i, 0)),
        out_shape=jax.ShapeDtypeStruct((n, d), F32),
        compiler_params=_cp("parallel"), name="final_norm",
    )(x2, g)


def _rope_tables(pos):
    half = HEAD_DIM // 2
    inv = ROPE_THETA ** (-jnp.arange(half, dtype=F32) / half)
    ang = pos.astype(F32)[:, None] * inv[None, :]
    cos = jnp.cos(ang)
    sin = jnp.sin(ang)
    return jnp.tile(cos, (1, 4)), jnp.concatenate([-sin, sin, -sin, sin], axis=1)


def _pad_w_in(w_in):
    g0 = A_WIDTH + 3 * KV_W
    g1 = g0 + 3 * A_HEADS
    pad = jnp.zeros(w_in.shape[:2] + (GATE_PAD - 3 * A_HEADS,), w_in.dtype)
    return jnp.concatenate([w_in[..., :g1], pad, w_in[..., g1:]], axis=-1)


def _compress_weights(pe, pool, w1, w2):
    rep = lambda a: jnp.concatenate([a[0], a[0], a[1], a[1]], axis=-1)
    z1 = jnp.zeros_like(w1[0])
    z2 = jnp.zeros_like(w2[0])
    w1bd = jnp.concatenate([jnp.concatenate([w1[0], z1, z1, z1], 1), jnp.concatenate([z1, w1[0], z1, z1], 1),
                            jnp.concatenate([z1, z1, w1[1], z1], 1), jnp.concatenate([z1, z1, z1, w1[1]], 1)], 0)
    w2bd = jnp.concatenate([jnp.concatenate([w2[0], z2, z2, z2], 1), jnp.concatenate([z2, w2[0], z2, z2], 1),
                            jnp.concatenate([z2, z2, w2[1], z2], 1), jnp.concatenate([z2, z2, z2, w2[1]], 1)], 0)
    return rep(pe), rep(pool), w1bd.astype(BF16), w2bd.astype(BF16)


def kernel(x_prompt, x_sample, cache_nsa_cmp_kv, cache_nsa_sel_kv, cache_nsa_win_kv, state_hgrn, cache_moba_kv, state_ffn_conv, page_table, c_prompt, c_sample, w_ada, b_ada, norm1_g, norm2_g, w_in, w_out, cmp_pe, cmp_pool, cmp_w1, cmp_w2, hgrn_lb_logits, hgrn_norm_g, w_up, conv_w, conv_b, w_down, final_g):
    bp, tp, d = x_prompt.shape
    bs, ts, _ = x_sample.shape
    depth = w_in.shape[0]
    d_ff = w_down.shape[1]
    n_pool, page = cache_moba_kv.shape[1], cache_moba_kv.shape[2]
    past = page_table.shape[1] * page
    wb = cache_nsa_win_kv.shape[2]
    np_, ns_ = bp * tp, bs * ts

    w_ada_bf = w_ada.astype(BF16)
    w_in_bf = _pad_w_in(w_in).astype(BF16)
    w_out_bf = w_out.astype(BF16)
    w_up_bf = w_up.astype(BF16)
    w_dn_bf = w_down.astype(BF16)
    ng4 = jnp.tile(hgrn_norm_g, (1, B_HEADS))

    mod = _modulation(jnp.concatenate([c_prompt, c_sample], axis=0), w_ada_bf, b_ada)
    mod = mod.reshape(depth, bp + bs, 6, d)

    cos_p, sin_p = _rope_tables(jnp.arange(tp))
    cos_s, sin_s = _rope_tables(past + jnp.arange(ts))
    cos_s, sin_s = jnp.tile(cos_s, (bs, 1)), jnp.tile(sin_s, (bs, 1))

    cmp_pages = cache_nsa_cmp_kv.reshape(depth * n_pool, page, KV_W)
    sel_pages = cache_nsa_sel_kv.reshape(depth * n_pool, page, KV_W)
    moba_pages = cache_moba_kv.reshape(depth * n_pool, page, 2 * C_WIDTH)
    win_cache = cache_nsa_win_kv.reshape(depth * bs, wb, KV_W)

    tm_p = min(512, tp)
    tm_f = min(256, tp)
    tm_fs = min(128, ns_)
    xp = x_prompt.reshape(np_, d)
    xs = x_sample.reshape(ns_, d)
    st_p, st_s = [], []
    zero_state = jnp.zeros((bp, B_W, B_W), F32)
    for l in range(depth):
        mp = [mod[l, :bp, j].reshape(bp, 1, d) for j in range(6)]
        ms = [jnp.repeat(mod[l, bp:, j], ts, axis=0) for j in range(6)]
        n1 = norm1_g[l].reshape(1, d)
        n2 = norm2_g[l].reshape(1, d)
        cw = _compress_weights(cmp_pe[l], cmp_pool[l], cmp_w1[l], cmp_w2[l])

        qa, qar, cmp_r, sel_r, win_r, gate, hg, cq, moba_r = _in_projection(
            xp, mp[1], mp[0], n1, w_in_bf[l], cos_p, sin_p, tm=tm_p, rows_per_mod=tp, rows_per_pos=tp)
        o_a = _nsa_prompt(qa, qar, gate, cmp_r, sel_r, win_r, cw, batch=bp, seq=tp)
        o_b, st_new = _hgrn(hg, zero_state, hgrn_lb_logits, ng4[l:l + 1], layer=l, batch=bp, seq=tp)
        o_c = _moba_prompt(cq, moba_r, batch=bp, seq=tp)
        xp, conv_st = _ffn(xp, o_a, o_b, o_c, (mp[2], mp[4], mp[3], mp[5]), n2, w_out_bf[l], w_up_bf[l],
                           conv_w[l], conv_b[l].reshape(1, -1), w_dn_bf[l], None, tm=tm_f, seq=tp, rows_per_mod=tp)
        w_keep = min(WINDOW, tp)
        st_p.append((cmp_r.reshape(bp, tp, 2, A_KV_HEADS, HEAD_DIM),
                     sel_r.reshape(bp, tp, 2, A_KV_HEADS, HEAD_DIM),
                     win_r.reshape(bp, tp, 2, A_KV_HEADS, HEAD_DIM)[:, tp - w_keep:],
                     _blockdiag_to_state(st_new),
                     moba_r.reshape(bp, tp, 2, C_HEADS, HEAD_DIM),
                     conv_st))

        qa, qar, cmp_r, sel_r, win_r, gate, hg, cq, moba_r = _in_projection(
            xs, ms[1], ms[0], n1, w_in_bf[l], cos_s, sin_s, tm=ns_, rows_per_mod=1, rows_per_pos=ns_)
        o_a, win_state = _nsa_sample(qa, qar, gate, sel_r, win_r, cmp_pages, sel_pages, win_cache, page_table, cw,
                                     layer=l, batch=bs, seq=ts)
        o_b, st_new = _hgrn(hg, _state_to_blockdiag(state_hgrn[l]), hgrn_lb_logits, ng4[l:l + 1],
                            layer=l, batch=bs, seq=ts)
        o_c = _moba_sample(cq, moba_r, moba_pages, page_table, layer=l, depth=depth, batch=bs, seq=ts)
        prev = state_ffn_conv[l]
        p0 = jnp.repeat(prev[:, 0], ts, axis=0)
        p1 = jnp.repeat(prev[:, 1], ts, axis=0)
        xs, u_s = _ffn(xs, o_a, o_b, o_c, (ms[2], ms[4], ms[3], ms[5]), n2, w_out_bf[l], w_up_bf[l],
                       conv_w[l], conv_b[l].reshape(1, -1), w_dn_bf[l], (p0, p1), tm=tm_fs, seq=ts, rows_per_mod=1)
        full = jnp.concatenate([prev, u_s.reshape(bs, ts, 2 * d_ff)], axis=1)
        st_s.append((cmp_r.reshape(bs, ts, 2, A_KV_HEADS, HEAD_DIM),
                     sel_r.reshape(bs, ts, 2, A_KV_HEADS, HEAD_DIM),
                     win_state.reshape(bs, -1, 2, A_KV_HEADS, HEAD_DIM),
                     _blockdiag_to_state(st_new),
                     moba_r.reshape(bs, ts, 2, C_HEADS, HEAD_DIM),
                     full[:, -(CONV_W - 1):]))

    fg = final_g.reshape(1, d)
    y_p = _final_norm(xp, fg, tm_p).reshape(bp, tp, d)
    y_s = _final_norm(xs, fg, ns_).reshape(bs, ts, d)

    def stacked(sts, j):
        return jnp.stack([st[j] for st in sts], axis=0)

    return (y_p, y_s,
            stacked(st_p, 0), stacked(st_p, 1), stacked(st_p, 2), stacked(st_p, 3), stacked(st_p, 4), stacked(st_p, 5),
            stacked(st_s, 0), stacked(st_s, 1), stacked(st_s, 2), stacked(st_s, 3), stacked(st_s, 4), stacked(st_s, 5))
```

```python
import functools

import jax
import jax.numpy as jnp
from jax import lax
from jax.experimental import pallas as pl
from jax.experimental.pallas import tpu as pltpu

F32 = jnp.float32
BF16 = jnp.bfloat16

HEAD_DIM = 64
ROPE_THETA = 10000.0
EPS = 1e-6
NEG = -1e30
A_HEADS = 6
A_KV_HEADS = 2
A_REP = A_HEADS // A_KV_HEADS
CMP_LEN = 32
CMP_STRIDE = 16
CMP_HIDDEN = 128
SEL_BLOCK = 64
SEL_TOP = 16
FORCE_BONUS = 1e4
WINDOW = 512
B_HEADS = 4
B_DK = 64
B_DV = 64
C_HEADS = 6
MOBA_BLOCK = 256
MOBA_TOP = 3
CONV_W = 3

LANES = 128
HG_CHUNK = 16
Q_TILE = 128
VMEM_LIMIT = 56 * 1024 * 1024

A_WIDTH = A_HEADS * HEAD_DIM
KV_W = 2 * A_KV_HEADS * HEAD_DIM
B_W = B_HEADS * B_DK
C_WIDTH = C_HEADS * HEAD_DIM
GATE_PAD = LANES
RO_QA = 0
RO_GATE = RO_QA + A_WIDTH
RO_HG = RO_GATE + GATE_PAD
RO_CQ = RO_HG + 4 * B_W
ROW_W = RO_CQ + C_WIDTH
TO_CMP = 0
TO_SEL = TO_CMP + KV_W
TO_WIN = TO_SEL + KV_W
TO_MK = TO_WIN + KV_W
TO_MV = TO_MK + C_WIDTH
KVT_W = TO_MV + C_WIDTH
ATT_SCALE = HEAD_DIM ** -0.5
KEY_CHUNK = 512


def _cp(*sem):
    return pltpu.CompilerParams(dimension_semantics=sem, vmem_limit_bytes=VMEM_LIMIT)


def _dot(a, b):
    return jnp.dot(a.astype(BF16), b.astype(BF16), preferred_element_type=F32)


def _dot_nt(a, b):
    return lax.dot_general(a.astype(BF16), b.astype(BF16), (((1,), (1,)), ((), ())),
                           preferred_element_type=F32)


def _dot_tn(a, b):
    return lax.dot_general(a.astype(BF16), b.astype(BF16), (((0,), (0,)), ((), ())),
                           preferred_element_type=F32)


def _split2(x):
    hi = x.astype(BF16)
    lo = (x - hi.astype(F32)).astype(BF16)
    return hi, lo


def _dot_x2(x, w_bf16):
    hi, lo = _split2(x)
    return (jnp.dot(hi, w_bf16, preferred_element_type=F32)
            + jnp.dot(lo, w_bf16, preferred_element_type=F32))


def _sigmoid(x):
    return 1.0 / (1.0 + jnp.exp(-x))


def _silu(x):
    return x * _sigmoid(x)


def _masked_softmax(s, mask):
    s = jnp.where(mask, s, NEG)
    m = jnp.max(s, axis=-1, keepdims=True)
    p = jnp.where(mask, jnp.exp(s - m), 0.0)
    return p / jnp.maximum(jnp.sum(p, axis=-1, keepdims=True), 1e-30)


def _iota(shape, dim):
    return lax.broadcasted_iota(jnp.int32, shape, dim)


def _rope128(x, cos, sin):
    lane = _iota(x.shape, 1)
    first = (lane & (HEAD_DIM - 1)) < (HEAD_DIM // 2)
    swapped = jnp.where(first, pltpu.roll(x, LANES - HEAD_DIM // 2, 1), pltpu.roll(x, HEAD_DIM // 2, 1))
    return x * cos + swapped * sin


def _rank_before(score, n):
    lane = _iota((1, score.shape[1]), 1)
    rank = jnp.zeros(score.shape, jnp.int32)
    for j in range(n):
        col = score[:, j:j + 1]
        beats = (col > score) | ((col == score) & (lane > j))
        rank = rank + beats.astype(jnp.int32)
    return rank


def _mod_kernel(c_ref, w_ref, b_ref, o_ref):
    o_ref[...] = _dot(_silu(c_ref[...]), w_ref[...]) + b_ref[...]


def _modulation(c_all, w_ada_bf, b_ada):
    depth, d, n6 = w_ada_bf.shape
    nb = c_all.shape[0]
    tn = n6 // 4
    return pl.pallas_call(
        _mod_kernel,
        grid=(depth, n6 // tn),
        in_specs=[pl.BlockSpec((nb, d), lambda l, j: (0, 0)),
                  pl.BlockSpec((None, d, tn), lambda l, j: (l, 0, j)),
                  pl.BlockSpec((None, 1, tn), lambda l, j: (l, 0, j))],
        out_specs=pl.BlockSpec((None, nb, tn), lambda l, j: (l, 0, j)),
        out_shape=jax.ShapeDtypeStruct((depth, nb, n6), F32),
        compiler_params=_cp("parallel", "parallel"),
        name="adaln_mod",
    )(c_all, w_ada_bf, b_ada.reshape(depth, 1, n6))


def _inproj_kernel(x_ref, sc_ref, sh_ref, g_ref, wrow_ref, wkvt_ref, cos_ref, sin_ref, cost_ref, sint_ref,
                   qa_ref, qar_ref, gate_ref, hg_ref, cq_ref, cmpt_ref, selt_ref, wint_ref, mobat_ref):
    x = x_ref[...]
    h = x * lax.rsqrt(jnp.mean(x * x, axis=-1, keepdims=True) + EPS) * g_ref[...]
    h = (h * (1.0 + sc_ref[...]) + sh_ref[...]).astype(BF16)
    cos = cos_ref[...]
    sin = sin_ref[...]
    cost = cost_ref[...]
    sint = sint_ref[...]

    def proj(c0, n):
        return jnp.dot(h, wrow_ref[:, c0:c0 + n], preferred_element_type=F32)

    def proj_t(r0, n):
        return lax.dot_general(wkvt_ref[r0:r0 + n, :], h, (((1,), (1,)), ((), ())), preferred_element_type=F32)

    def rope_cols(y):
        return jnp.concatenate([_rope128(y[:, j:j + LANES], cos, sin) for j in range(0, y.shape[1], LANES)], axis=1)

    def rope_rows(y):
        hh = HEAD_DIM // 2
        parts = []
        for r0 in range(0, y.shape[0], HEAD_DIM):
            x1 = y[r0:r0 + hh, :]
            x2 = y[r0 + hh:r0 + HEAD_DIM, :]
            parts += [x1 * cost - x2 * sint, x2 * cost + x1 * sint]
        return jnp.concatenate(parts, axis=0)

    qa = proj(RO_QA, A_WIDTH)
    qa_ref[...] = qa
    qar_ref[...] = rope_cols(qa)
    gate_ref[...] = proj(RO_GATE, GATE_PAD)
    hg_ref[...] = proj(RO_HG, 4 * B_W)
    cq_ref[...] = rope_cols(proj(RO_CQ, C_WIDTH))
    half = KV_W // 2
    cmpt_ref[...] = proj_t(TO_CMP, KV_W)
    selt_ref[0:half, :] = rope_rows(proj_t(TO_SEL, half))
    selt_ref[half:, :] = proj_t(TO_SEL + half, half)
    wint_ref[0:half, :] = rope_rows(proj_t(TO_WIN, half))
    wint_ref[half:, :] = proj_t(TO_WIN + half, half)
    mobat_ref[0:C_WIDTH, :] = rope_rows(proj_t(TO_MK, C_WIDTH))
    mobat_ref[C_WIDTH:, :] = proj_t(TO_MV, C_WIDTH)


def _in_projection(x2, sc, sh, gain, w_row, w_kvt, cos, sin, cost, sint, *, tm, seq, rows_per_mod, rows_per_pos):
    n, d = x2.shape
    assert n % seq == 0 and seq % tm == 0 and rows_per_pos % tm == 0
    nseq = n // seq
    tps = seq // tm
    pos_blocks = rows_per_pos // tm
    if rows_per_mod > 1:
        per = rows_per_mod // tm
        mod_spec = pl.BlockSpec((None, 1, d), lambda i: (i // per, 0, 0))
    else:
        mod_spec = pl.BlockSpec((tm, d), lambda i: (i, 0))
    row_w = (A_WIDTH, A_WIDTH, GATE_PAD, 4 * B_W, C_WIDTH)
    t_w = (KV_W, KV_W, KV_W, 2 * C_WIDTH)
    full = lambda a: pl.BlockSpec(a.shape, lambda i: (0,) * a.ndim)
    return pl.pallas_call(
        _inproj_kernel,
        grid=(n // tm,),
        in_specs=[pl.BlockSpec((tm, d), lambda i: (i, 0)), mod_spec, mod_spec, full(gain), full(w_row), full(w_kvt),
                  pl.BlockSpec((tm, LANES), lambda i: (i % pos_blocks, 0)),
                  pl.BlockSpec((tm, LANES), lambda i: (i % pos_blocks, 0)),
                  pl.BlockSpec((HEAD_DIM // 2, tm), lambda i: (0, i % pos_blocks)),
                  pl.BlockSpec((HEAD_DIM // 2, tm), lambda i: (0, i % pos_blocks))],
        out_specs=[pl.BlockSpec((tm, wd), lambda i: (i, 0)) for wd in row_w]
        + [pl.BlockSpec((None, wd, tm), lambda i: (i // tps, 0, i % tps)) for wd in t_w],
        out_shape=[jax.ShapeDtypeStruct((n, wd), F32) for wd in row_w]
        + [jax.ShapeDtypeStruct((nseq, wd, seq), F32) for wd in t_w],
        compiler_params=_cp("parallel"),
        name="in_projection",
    )(x2, sc, sh, gain, w_row, w_kvt, cos, sin, cost, sint)


def _hgrn_kernel(hg_ref, st0_ref, lbl_ref, ng_ref, o_ref, st_out_ref,
                 st_scr, b_scr, key_scr, q_scr, v_scr, g_scr, o_scr, *, layer, valid, rows, nt):
    t = pl.program_id(1)

    @pl.when(t == 0)
    def _():
        st_scr[...] = st0_ref[...]

    lg = lbl_ref[...]
    e = jnp.exp(lg - jnp.max(lg, axis=0, keepdims=True))
    p = e / jnp.sum(e, axis=0, keepdims=True)
    acc = p[0:1, :]
    for i in range(1, layer + 1):
        acc = acc + p[i:i + 1, :]
    lb = jnp.clip(acc - p[0:1, :], 0.0, 1.0)
    lbpos = lb > 0.0

    if valid < rows:
        g_scr[...] = jnp.zeros_like(g_scr)
        v_scr[...] = jnp.zeros_like(v_scr)
        q_scr[...] = jnp.zeros_like(q_scr)
        key_scr[...] = jnp.zeros_like(key_scr)
        q_scr[0:valid, :] = hg_ref[:, 0:B_W]
        key_scr[0:valid, :] = hg_ref[:, B_W:2 * B_W]
        v_scr[0:valid, :] = hg_ref[:, 2 * B_W:3 * B_W]
        g_scr[0:valid, :] = hg_ref[:, 3 * B_W:4 * B_W]
        q = q_scr[...]
        z = key_scr[...]
    else:
        q = hg_ref[:, 0:B_W]
        z = hg_ref[:, B_W:2 * B_W]
        v_scr[...] = hg_ref[:, 2 * B_W:3 * B_W]
        g_scr[...] = hg_ref[:, 3 * B_W:4 * B_W]

    ls = jnp.minimum(z, 0.0) - jnp.log1p(jnp.exp(-jnp.abs(z)))
    c_ = jnp.log1p(-lb) + ls
    a_ = jnp.log(jnp.where(lbpos, lb, 1.0))
    logf = jnp.where(lbpos, jnp.maximum(a_, c_) + jnp.log1p(jnp.exp(-jnp.abs(a_ - c_))), c_)
    key = (1.0 - lb) * _sigmoid(-z)
    rowi = _iota((rows, 1), 0)
    live = rowi < valid
    logf = jnp.where(live, logf, 0.0)
    key = jnp.where(live, key, 0.0)
    ri = _iota((rows, rows), 0)
    ci = _iota((rows, rows), 1)
    tri = jnp.where(((ri // HG_CHUNK) == (ci // HG_CHUNK)) & (ci <= ri), 1.0, 0.0).astype(BF16)
    h1 = logf.astype(BF16)
    r1 = logf - h1.astype(F32)
    h2 = r1.astype(BF16)
    h3 = (r1 - h2.astype(F32)).astype(BF16)
    b_scr[...] = (jnp.dot(tri, h1, preferred_element_type=F32) + jnp.dot(tri, h2, preferred_element_type=F32)
                  + jnp.dot(tri, h3, preferred_element_type=F32))
    key_scr[...] = key
    q_scr[...] = _silu(q)

    hr = _iota((B_W, B_W), 0)
    hc = _iota((B_W, B_W), 1)
    same_head = (hr // B_DK) == (hc // B_DK)
    head_ones = jnp.where(same_head, 1.0, 0.0).astype(BF16)
    ti = _iota((HG_CHUNK, 1), 0)
    ng = ng_ref[...]

    def chunk(c, carry):
        r0 = pl.multiple_of(c * HG_CHUNK, HG_CHUNK)
        b = b_scr[pl.ds(r0, HG_CHUNK), :]
        kk = key_scr[pl.ds(r0, HG_CHUNK), :]
        qf = q_scr[pl.ds(r0, HG_CHUNK), :]
        v = v_scr[pl.ds(r0, HG_CHUNK), :]
        gt = g_scr[pl.ds(r0, HG_CHUNK), :]
        st = st_scr[...]
        blast = b[HG_CHUNK - 1:HG_CHUNK, :]
        o_inter = _dot_nt(qf * jnp.exp(b), st)
        es = []
        for s in range(HG_CHUNK):
            e_s = qf * kk[s:s + 1, :] * jnp.exp(jnp.minimum(b - b[s:s + 1, :], 0.0))
            es.append(jnp.where(ti >= s, e_s, 0.0))
        r = _dot_x2(jnp.concatenate(es, axis=0), head_ones)
        o = o_inter
        for s in range(HG_CHUNK):
            o = o + r[s * HG_CHUNK:(s + 1) * HG_CHUNK, :] * v[s:s + 1, :]
        ms = _dot_x2(o * o, head_ones) * (1.0 / B_DV)
        o_scr[pl.ds(r0, HG_CHUNK), :] = o * lax.rsqrt(ms + EPS) * ng * _silu(gt)
        upd = _dot_tn(v, kk * jnp.exp(blast - b))
        st_scr[...] = st * jnp.exp(blast) + jnp.where(same_head, upd, 0.0)
        return carry

    lax.fori_loop(0, rows // HG_CHUNK, chunk, 0)
    o_ref[...] = o_scr[0:valid, :]

    @pl.when(t == nt - 1)
    def _():
        st_out_ref[...] = st_scr[...]


def _hgrn(hg, st0, lb_logits, norm_g4, *, layer, batch, seq):
    if seq % HG_CHUNK == 0:
        tt = min(seq, 256)
        assert seq % tt == 0
        nt = seq // tt
        rows = valid = tt
        hg_in = hg
        hg_spec = pl.BlockSpec((tt, 4 * B_W), lambda b, t: (b * nt + t, 0))
        o_spec = pl.BlockSpec((tt, B_W), lambda b, t: (b * nt + t, 0))
        o_shape = jax.ShapeDtypeStruct((batch * seq, B_W), F32)
    else:
        assert seq < HG_CHUNK
        nt, rows, valid = 1, HG_CHUNK, seq
        hg_in = hg.reshape(batch, seq, 4 * B_W)
        hg_spec = pl.BlockSpec((None, seq, 4 * B_W), lambda b, t: (b, 0, 0))
        o_spec = pl.BlockSpec((None, seq, B_W), lambda b, t: (b, 0, 0))
        o_shape = jax.ShapeDtypeStruct((batch, seq, B_W), F32)
    depth = lb_logits.shape[0]
    kern = functools.partial(_hgrn_kernel, layer=layer, valid=valid, rows=rows, nt=nt)
    o, st = pl.pallas_call(
        kern,
        grid=(batch, nt),
        in_specs=[hg_spec,
                  pl.BlockSpec((None, B_W, B_W), lambda b, t: (b, 0, 0)),
                  pl.BlockSpec((depth, B_W), lambda b, t: (0, 0)),
                  pl.BlockSpec((1, B_W), lambda b, t: (0, 0))],
        out_specs=[o_spec, pl.BlockSpec((None, B_W, B_W), lambda b, t: (b, 0, 0))],
        out_shape=[o_shape, jax.ShapeDtypeStruct((batch, B_W, B_W), F32)],
        scratch_shapes=[pltpu.VMEM((B_W, B_W), F32)] + [pltpu.VMEM((rows, B_W), F32)] * 6,
        compiler_params=_cp("parallel", "arbitrary"),
        name="hgrn2",
    )(hg_in, st0, lb_logits, norm_g4)
    return o.reshape(batch * seq, B_W), st


def _state_to_blockdiag(s0):
    b = s0.shape[0]
    eye = jnp.eye(B_HEADS, dtype=s0.dtype)
    return jnp.einsum('bhkv,hg->bhvgk', s0, eye).reshape(b, B_W, B_W)


def _blockdiag_to_state(st):
    b = st.shape[0]
    eye = jnp.eye(B_HEADS, dtype=st.dtype)
    return jnp.einsum('bhvgk,hg->bhkv', st.reshape(b, B_HEADS, B_DV, B_HEADS, B_DK), eye)


def _tile_rows(x, n):
    return jnp.concatenate([x] * n, axis=0)


def _softmax_pv(q, kt, vt, bias):
    s = _dot(q, kt) * ATT_SCALE + bias
    p = jnp.exp(s - jnp.max(s, axis=-1, keepdims=True))
    return _dot_nt(p, vt) / jnp.sum(p, axis=-1, keepdims=True)


def _softmax_pv2(q, kt1, vt1, bias1, kt2, vt2, bias2):
    s1 = _dot(q, kt1) * ATT_SCALE + bias1
    s2 = _dot(q, kt2) * ATT_SCALE + bias2
    m = jnp.maximum(jnp.max(s1, axis=-1, keepdims=True), jnp.max(s2, axis=-1, keepdims=True))
    p1 = jnp.exp(s1 - m)
    p2 = jnp.exp(s2 - m)
    den = jnp.sum(p1, axis=-1, keepdims=True) + jnp.sum(p2, axis=-1, keepdims=True)
    return (_dot_nt(p1, vt1) + _dot_nt(p2, vt2)) / den


def _compress_t(xt, pa, pb, pet, poolt, w1t, w2t, nblk):
    length = xt.shape[1]
    tb = _iota((length, nblk), 0) // CMP_STRIDE
    nb = _iota((length, nblk), 1)
    first = jnp.where(tb == nb, 1.0, 0.0).astype(BF16)
    second = jnp.where(tb == nb + 1, 1.0, 0.0).astype(BF16)
    bias = jnp.sum(pet * poolt, axis=1, keepdims=True)
    pooled = _dot_x2(xt * pa, first) + _dot_x2(xt * pb, second) + bias
    hid = _silu(jnp.dot(w1t, pooled.astype(BF16), preferred_element_type=F32))
    return jnp.dot(w2t, hid.astype(BF16), preferred_element_type=F32)


def _nsa_select(qa, cmpt, qpos, *, n_cmp, n_sel):
    tq = qa.shape[0]
    ncp = cmpt.shape[1]
    n_i = _iota((1, ncp), 1)
    maskf = jnp.where((n_i * CMP_STRIDE + (CMP_LEN - 1) <= qpos) & (n_i < n_cmp), 1.0, 0.0)
    mask3 = _tile_rows(maskf, A_REP) > 0.5
    on = _iota((ncp, LANES), 0) * CMP_STRIDE
    os_ = _iota((ncp, LANES), 1) * SEL_BLOCK
    overlap = jnp.where((on < os_ + SEL_BLOCK) & (on + CMP_LEN > os_), 1.0, 0.0).astype(BF16)
    s_i = _iota((1, LANES), 1)
    cur = qpos // SEL_BLOCK
    forced = (s_i == 0) | (s_i == cur) | (s_i == cur - 1)
    visible = (s_i * SEL_BLOCK <= qpos) & (s_i < n_sel)
    half = KV_W // 2
    o_cmp, chosen = [], []
    for g in range(A_KV_HEADS):
        kct = cmpt[g * HEAD_DIM:(g + 1) * HEAD_DIM, :]
        vct = cmpt[half + g * HEAD_DIM:half + (g + 1) * HEAD_DIM, :]
        q3 = _stack_heads(qa, g)
        p = _masked_softmax(_dot(q3, kct) * ATT_SCALE, mask3)
        o_cmp.append(_dot_nt(p, vct))
        p_sum = p[0:tq]
        for r in range(1, A_REP):
            p_sum = p_sum + p[r * tq:(r + 1) * tq]
        imp = _dot_x2(p_sum, overlap)
        score = jnp.where(visible, imp + jnp.where(forced, FORCE_BONUS, 0.0), NEG)
        chosen.append(jnp.where(_rank_before(score, n_sel) < min(SEL_TOP, n_sel), 1.0, 0.0))
    return o_cmp, chosen


def _stack_heads(q, g):
    return jnp.concatenate([q[:, (g * A_REP + r) * HEAD_DIM:(g * A_REP + r + 1) * HEAD_DIM] for r in range(A_REP)], axis=0)


def _compress_kernel(xt_ref, pa_ref, pb_ref, pet_ref, poolt_ref, w1t_ref, w2t_ref, o_ref, *, nblk):
    o_ref[...] = _compress_t(xt_ref[...], pa_ref[...], pb_ref[...], pet_ref[...], poolt_ref[...],
                             w1t_ref[...], w2t_ref[...], nblk)


def _nsa_prompt_kernel(qa_ref, qar_ref, gate_ref, cmp_ref, selt_ref, wint_ref, o_ref, part_scr, ch_scr,
                       *, seq, tq, n_cmp, n_sel, span, kchunk):
    q0 = pl.program_id(1) * tq
    qpos = q0 + _iota((tq, 1), 0)
    gs = _sigmoid(gate_ref[...])
    qar = qar_ref[...]
    o_cmp, chosen = _nsa_select(qa_ref[...], cmp_ref[...], qpos, n_cmp=n_cmp, n_sel=n_sel)
    start = pl.multiple_of(jnp.clip(q0 - WINDOW, 0, seq - span), LANES)
    kpos_w = start + _iota((1, span), 1)
    wbias = _tile_rows(jnp.where((kpos_w <= qpos) & (qpos - kpos_w <= WINDOW), 0.0, NEG), A_REP)
    half = KV_W // 2
    parts = []
    for g in range(A_KV_HEADS):
        q3 = _stack_heads(qar, g)
        o_win = _softmax_pv(q3, wint_ref[g * HEAD_DIM:(g + 1) * HEAD_DIM, pl.ds(start, span)],
                            wint_ref[half + g * HEAD_DIM:half + (g + 1) * HEAD_DIM, pl.ds(start, span)], wbias)
        for r in range(A_REP):
            c0 = 3 * (g * A_REP + r)
            parts.append(gs[:, c0:c0 + 1] * o_cmp[g][r * tq:(r + 1) * tq] + gs[:, c0 + 2:c0 + 3] * o_win[r * tq:(r + 1) * tq])
        ch_scr[g] = chosen[g]
    part_scr[...] = jnp.concatenate(parts, axis=1)
    bucket = (q0 + tq - 1) // kchunk
    for k in range(seq // kchunk):
        @pl.when(bucket == k)
        def _(k=k):
            klen = (k + 1) * kchunk
            expand = jnp.where((_iota((LANES, klen), 1) // SEL_BLOCK) == _iota((LANES, klen), 0), 1.0, 0.0).astype(BF16)
            causal = _iota((1, klen), 1) <= qpos
            outs = []
            for g in range(A_KV_HEADS):
                ck = jnp.dot(ch_scr[g].astype(BF16), expand, preferred_element_type=F32)
                bias = _tile_rows(jnp.where((ck > 0.5) & causal, 0.0, NEG), A_REP)
                o_sel = _softmax_pv(_stack_heads(qar, g), selt_ref[g * HEAD_DIM:(g + 1) * HEAD_DIM, 0:klen],
                                    selt_ref[half + g * HEAD_DIM:half + (g + 1) * HEAD_DIM, 0:klen], bias)
                for r in range(A_REP):
                    c0 = 3 * (g * A_REP + r)
                    outs.append(gs[:, c0 + 1:c0 + 2] * o_sel[r * tq:(r + 1) * tq])
            o_ref[...] = part_scr[...] + jnp.concatenate(outs, axis=1)


def _nsa_prompt(qa, qar, gate, cmpt, selt, wint, cw, *, batch, seq):
    pa, pb, pet, poolt, w1t, w2t = cw
    assert seq % Q_TILE == 0 and seq % CMP_STRIDE == 0 and seq >= CMP_LEN
    nblk = seq // CMP_STRIDE
    n_cmp = (seq - CMP_LEN) // CMP_STRIDE + 1
    n_sel = -(-seq // SEL_BLOCK)
    assert nblk <= LANES and n_sel <= LANES and n_cmp == nblk - 1
    tq = Q_TILE
    nq = seq // tq
    span = min(seq, tq + WINDOW)
    kchunk = min(KEY_CHUNK, seq)
    assert seq % kchunk == 0 and kchunk % tq == 0
    full = lambda a: pl.BlockSpec(a.shape, lambda b: (0,) * a.ndim)
    cmpkv = pl.pallas_call(
        functools.partial(_compress_kernel, nblk=nblk),
        grid=(batch,),
        in_specs=[pl.BlockSpec((None, KV_W, seq), lambda b: (b, 0, 0)),
                  full(pa), full(pb), full(pet), full(poolt), full(w1t), full(w2t)],
        out_specs=pl.BlockSpec((None, KV_W, nblk), lambda b: (b, 0, 0)),
        out_shape=jax.ShapeDtypeStruct((batch, KV_W, nblk), F32),
        compiler_params=_cp("parallel"),
        name="nsa_compress",
    )(cmpt, pa, pb, pet, poolt, w1t, w2t)
    kern = functools.partial(_nsa_prompt_kernel, seq=seq, tq=tq, n_cmp=n_cmp, n_sel=n_sel, span=span, kchunk=kchunk)
    return pl.pallas_call(
        kern,
        grid=(batch, nq),
        in_specs=[pl.BlockSpec((tq, A_WIDTH), lambda b, i: (b * nq + i, 0)),
                  pl.BlockSpec((tq, A_WIDTH), lambda b, i: (b * nq + i, 0)),
                  pl.BlockSpec((tq, GATE_PAD), lambda b, i: (b * nq + i, 0)),
                  pl.BlockSpec((None, KV_W, nblk), lambda b, i: (b, 0, 0)),
                  pl.BlockSpec((None, KV_W, seq), lambda b, i: (b, 0, 0)),
                  pl.BlockSpec((None, KV_W, seq), lambda b, i: (b, 0, 0))],
        out_specs=pl.BlockSpec((tq, A_WIDTH), lambda b, i: (b * nq + i, 0)),
        out_shape=jax.ShapeDtypeStruct((batch * seq, A_WIDTH), F32),
        scratch_shapes=[pltpu.VMEM((tq, A_WIDTH), F32), pltpu.VMEM((A_KV_HEADS, tq, LANES), F32)],
        compiler_params=_cp("parallel", "parallel"),
        name="nsa_prompt",
    )(qa, qar, gate, cmpkv, selt, wint)


def _nsa_sample_kernel(pt_ref, *refs, n_pages, page, seq, past, wb):
    cmp_pages = refs[:n_pages]
    sel_pages = refs[n_pages:2 * n_pages]
    (wcache_ref, qa_ref, qar_ref, gate_ref, stail_ref, wtail_ref, pa_ref, pb_ref, pet_ref, poolt_ref, w1t_ref, w2t_ref,
     o_ref, wstate_ref, cmp_scr, sel_scr, q_scr) = refs[2 * n_pages:]
    del pt_ref
    for j in range(n_pages):
        cmp_scr[:, j * page:(j + 1) * page] = cmp_pages[j][...]
        sel_scr[:, j * page:(j + 1) * page] = sel_pages[j][...]
    tq = q_scr.shape[1]
    q_scr[...] = jnp.zeros_like(q_scr)
    q_scr[0, 0:seq, :] = qa_ref[...]
    q_scr[1, 0:seq, :] = qar_ref[...]
    q_scr[2, 0:seq, 0:GATE_PAD] = gate_ref[...]
    nblk = past // CMP_STRIDE
    total = past + seq
    n_cmp = (total - CMP_LEN) // CMP_STRIDE + 1
    n_sel = -(-total // SEL_BLOCK)
    new_blk = past // SEL_BLOCK
    cmpt = _compress_t(cmp_scr[...], pa_ref[...], pb_ref[...], pet_ref[...], poolt_ref[...],
                       w1t_ref[...], w2t_ref[...], nblk)
    qpos = past + _iota((tq, 1), 0)
    o_cmp, chosen = _nsa_select(q_scr[0], cmpt, qpos, n_cmp=n_cmp, n_sel=n_sel)
    gs = _sigmoid(q_scr[2][:, 0:GATE_PAD])
    qar = q_scr[1]
    tnew = _iota((1, LANES), 1) - (LANES - seq)
    trow = _iota((tq, 1), 0)
    newok = (tnew >= 0) & (tnew <= trow) & (trow - tnew <= WINDOW)
    expand = jnp.where((_iota((LANES, past), 1) // SEL_BLOCK) == _iota((LANES, past), 0), 1.0, 0.0).astype(BF16)
    kpos_w = past - wb + _iota((1, wb), 1)
    wb1 = _tile_rows(jnp.where((kpos_w <= qpos) & (qpos - kpos_w <= WINDOW), 0.0, NEG), A_REP)
    wb2 = _tile_rows(jnp.where(newok, 0.0, NEG), A_REP)
    half = KV_W // 2
    outs = []
    for g in range(A_KV_HEADS):
        ks = slice(g * HEAD_DIM, (g + 1) * HEAD_DIM)
        vs = slice(half + g * HEAD_DIM, half + (g + 1) * HEAD_DIM)
        ck = jnp.dot(chosen[g].astype(BF16), expand, preferred_element_type=F32)
        sb1 = _tile_rows(jnp.where(ck > 0.5, 0.0, NEG), A_REP)
        sb2 = _tile_rows(jnp.where(newok & (chosen[g][:, new_blk:new_blk + 1] > 0.5), 0.0, NEG), A_REP)
        q3 = _stack_heads(qar, g)
        o_sel = _softmax_pv2(q3, sel_scr[ks, :], sel_scr[vs, :], sb1, stail_ref[ks, :], stail_ref[vs, :], sb2)
        o_win = _softmax_pv2(q3, wcache_ref[ks, :], wcache_ref[vs, :], wb1, wtail_ref[ks, :], wtail_ref[vs, :], wb2)
        for r in range(A_REP):
            c0 = 3 * (g * A_REP + r)
            rs = slice(r * tq, (r + 1) * tq)
            outs.append(gs[:, c0:c0 + 1] * o_cmp[g][rs] + gs[:, c0 + 1:c0 + 2] * o_sel[rs] + gs[:, c0 + 2:c0 + 3] * o_win[rs])
    o_ref[...] = jnp.concatenate(outs, axis=1)[0:seq, :]
    rolled = pltpu.roll(wcache_ref[...], wb - seq, 1)
    wstate_ref[:, 0:wb - LANES] = rolled[:, 0:wb - LANES]
    lane = _iota((1, LANES), 1)
    wstate_ref[:, wb - LANES:wb] = jnp.where(lane >= LANES - seq, wtail_ref[...], rolled[:, wb - LANES:wb])


def _nsa_sample(qa, qar, gate, stail, wtail, cmp_pages_t, sel_pages_t, win_cache_t, page_table, cw,
                *, layer, depth, batch, seq):
    pa, pb, pet, poolt, w1t, w2t = cw
    n_pages = page_table.shape[1]
    page = cmp_pages_t.shape[2]
    n_pool = cmp_pages_t.shape[0] // depth
    past = n_pages * page
    wb = win_cache_t.shape[2]
    total = past + seq
    nblk = past // CMP_STRIDE
    assert past % CMP_STRIDE == 0 and seq < CMP_STRIDE and past >= CMP_LEN and seq <= 8
    assert (total - CMP_LEN) // CMP_STRIDE + 1 == nblk - 1 and nblk <= LANES
    assert -(-total // SEL_BLOCK) <= LANES and past % SEL_BLOCK == 0 and page % LANES == 0
    assert wb == WINDOW and wb % LANES == 0 and wb > LANES
    base = layer * n_pool

    def page_spec(j):
        return pl.BlockSpec((None, KV_W, page), lambda b, pt: (base + pt[b, j], 0, 0))

    def tok_spec(wd):
        return pl.BlockSpec((None, seq, wd), lambda b, pt: (b, 0, 0))

    def full_spec(a):
        return pl.BlockSpec(a.shape, lambda b, pt: (0,) * a.ndim)

    tail_spec = pl.BlockSpec((None, KV_W, LANES), lambda b, pt: (b, 0, 0))
    kern = functools.partial(_nsa_sample_kernel, n_pages=n_pages, page=page, seq=seq, past=past, wb=wb)
    r3 = lambda a: a.reshape(batch, seq, a.shape[-1])
    o, wstate = pl.pallas_call(
        kern,
        grid_spec=pltpu.PrefetchScalarGridSpec(
            num_scalar_prefetch=1,
            grid=(batch,),
            in_specs=[page_spec(j) for j in range(n_pages)] + [page_spec(j) for j in range(n_pages)]
            + [pl.BlockSpec((None, KV_W, wb), lambda b, pt: (layer * batch + b, 0, 0)),
               tok_spec(A_WIDTH), tok_spec(A_WIDTH), tok_spec(GATE_PAD), tail_spec, tail_spec,
               full_spec(pa), full_spec(pb), full_spec(pet), full_spec(poolt), full_spec(w1t), full_spec(w2t)],
            out_specs=[tok_spec(A_WIDTH), pl.BlockSpec((None, KV_W, wb), lambda b, pt: (b, 0, 0))],
            scratch_shapes=[pltpu.VMEM((KV_W, past), F32), pltpu.VMEM((KV_W, past), F32),
                            pltpu.VMEM((3, 8, A_WIDTH), F32)]),
        out_shape=[jax.ShapeDtypeStruct((batch, seq, A_WIDTH), F32),
                   jax.ShapeDtypeStruct((batch, KV_W, wb), F32)],
        compiler_params=_cp("parallel"),
        name="nsa_sample",
    )(page_table, *([cmp_pages_t] * n_pages), *([sel_pages_t] * n_pages), win_cache_t,
      r3(qa), r3(qar), r3(gate), stail, wtail, pa, pb, pet, poolt, w1t, w2t)
    return o.reshape(batch * seq, A_WIDTH), wstate


def _moba_means_t(kt, n_keys):
    length = kt.shape[1]
    tb = _iota((length, LANES), 0) // MOBA_BLOCK
    nb = _iota((length, LANES), 1)
    bm = jnp.where((tb == nb) & (_iota((length, LANES), 0) < n_keys), 1.0 / MOBA_BLOCK, 0.0).astype(BF16)
    h1 = kt.astype(BF16)
    r1 = kt - h1.astype(F32)
    h2 = r1.astype(BF16)
    h3 = (r1 - h2.astype(F32)).astype(BF16)
    return (jnp.dot(h1, bm, preferred_element_type=F32) + jnp.dot(h2, bm, preferred_element_type=F32)
            + jnp.dot(h3, bm, preferred_element_type=F32))


def _moba_choose(qh, mt, past_ok, n_blocks):
    q_hi, q_lo = _split2(qh)
    m_hi, m_lo = _split2(mt)
    gate = (jnp.dot(q_hi, m_hi, preferred_element_type=F32) + jnp.dot(q_lo, m_hi, preferred_element_type=F32)
            + jnp.dot(q_hi, m_lo, preferred_element_type=F32))
    gate = jnp.where(past_ok, gate, NEG)
    return jnp.where(past_ok & (_rank_before(gate, n_blocks) < min(MOBA_TOP, n_blocks)), 1.0, 0.0)


def _moba_prompt_kernel(q_ref, kvt_ref, o_ref, means_scr, ch_scr, *, seq, tq, n_blocks, kchunk):
    @pl.when(pl.program_id(1) == 0)
    def _():
        means_scr[...] = _moba_means_t(kvt_ref[0:C_WIDTH, :], seq)

    q0 = pl.program_id(1) * tq
    qpos = q0 + _iota((tq, 1), 0)
    q = q_ref[...]
    q_blk = qpos // MOBA_BLOCK
    past_ok = _iota((1, LANES), 1) < q_blk
    for h in range(C_HEADS):
        hs = slice(h * HEAD_DIM, (h + 1) * HEAD_DIM)
        ch_scr[h] = _moba_choose(q[:, hs], means_scr[hs, :], past_ok, n_blocks)
    bucket = (q0 + tq - 1) // kchunk
    for k in range(seq // kchunk):
        @pl.when(bucket == k)
        def _(k=k):
            klen = (k + 1) * kchunk
            expand = jnp.where((_iota((LANES, klen), 1) // MOBA_BLOCK) == _iota((LANES, klen), 0), 1.0, 0.0).astype(BF16)
            kpos = _iota((1, klen), 1)
            own = ((kpos // MOBA_BLOCK) == q_blk) & (kpos <= qpos)
            outs = []
            for h in range(C_HEADS):
                hs = slice(h * HEAD_DIM, (h + 1) * HEAD_DIM)
                vs = slice(C_WIDTH + h * HEAD_DIM, C_WIDTH + (h + 1) * HEAD_DIM)
                ck = jnp.dot(ch_scr[h].astype(BF16), expand, preferred_element_type=F32)
                bias = jnp.where((ck > 0.5) | own, 0.0, NEG)
                outs.append(_softmax_pv(q[:, hs], kvt_ref[hs, 0:klen], kvt_ref[vs, 0:klen], bias))
            o_ref[...] = jnp.concatenate(outs, axis=1)


def _moba_prompt(cq, kvt, *, batch, seq):
    tq = Q_TILE
    nq = seq // tq
    n_blocks = -(-seq // MOBA_BLOCK)
    kchunk = min(KEY_CHUNK, seq)
    assert seq % tq == 0 and n_blocks <= LANES and seq % kchunk == 0 and kchunk % tq == 0
    kern = functools.partial(_moba_prompt_kernel, seq=seq, tq=tq, n_blocks=n_blocks, kchunk=kchunk)
    return pl.pallas_call(
        kern,
        grid=(batch, nq),
        in_specs=[pl.BlockSpec((tq, C_WIDTH), lambda b, i: (b * nq + i, 0)),
                  pl.BlockSpec((None, 2 * C_WIDTH, seq), lambda b, i: (b, 0, 0))],
        out_specs=pl.BlockSpec((tq, C_WIDTH), lambda b, i: (b * nq + i, 0)),
        out_shape=jax.ShapeDtypeStruct((batch * seq, C_WIDTH), F32),
        scratch_shapes=[pltpu.VMEM((C_WIDTH, LANES), F32), pltpu.VMEM((C_HEADS, tq, LANES), F32)],
        compiler_params=_cp("parallel", "arbitrary"),
        name="moba_prompt",
    )(cq, kvt)


def _moba_sample_kernel(pt_ref, *refs, n_pages, page, seq, past):
    pages = refs[:n_pages]
    q_ref, tail_ref, o_ref, kv_scr, q_scr = refs[n_pages:]
    del pt_ref
    for j in range(n_pages):
        kv_scr[:, j * page:(j + 1) * page] = pages[j][...]
    tq = q_scr.shape[0]
    q_scr[...] = jnp.zeros_like(q_scr)
    q_scr[0:seq, :] = q_ref[...]
    q = q_scr[...]
    n_blocks = -(-(past + seq) // MOBA_BLOCK)
    means = _moba_means_t(kv_scr[0:C_WIDTH, :], past)
    qpos = past + _iota((tq, 1), 0)
    past_ok = _iota((1, LANES), 1) < (qpos // MOBA_BLOCK)
    expand = jnp.where((_iota((LANES, past), 1) // MOBA_BLOCK) == _iota((LANES, past), 0), 1.0, 0.0).astype(BF16)
    tnew = _iota((1, LANES), 1) - (LANES - seq)
    trow = _iota((tq, 1), 0)
    b2 = jnp.where((tnew >= 0) & (tnew <= trow), 0.0, NEG)
    outs = []
    for h in range(C_HEADS):
        hs = slice(h * HEAD_DIM, (h + 1) * HEAD_DIM)
        vs = slice(C_WIDTH + h * HEAD_DIM, C_WIDTH + (h + 1) * HEAD_DIM)
        ch = _moba_choose(q[:, hs], means[hs, :], past_ok, n_blocks)
        ck = jnp.dot(ch.astype(BF16), expand, preferred_element_type=F32)
        b1 = jnp.where(ck > 0.5, 0.0, NEG)
        outs.append(_softmax_pv2(q[:, hs], kv_scr[hs, :], kv_scr[vs, :], b1, tail_ref[hs, :], tail_ref[vs, :], b2))
    o_ref[...] = jnp.concatenate(outs, axis=1)[0:seq, :]


def _moba_sample(cq, tail, pages_t, page_table, *, layer, depth, batch, seq):
    n_pages = page_table.shape[1]
    page = pages_t.shape[2]
    n_pool = pages_t.shape[0] // depth
    past = n_pages * page
    assert past % MOBA_BLOCK == 0 and seq <= 8 and -(-(past + seq) // MOBA_BLOCK) <= LANES and page % LANES == 0
    base = layer * n_pool
    kern = functools.partial(_moba_sample_kernel, n_pages=n_pages, page=page, seq=seq, past=past)
    o = pl.pallas_call(
        kern,
        grid_spec=pltpu.PrefetchScalarGridSpec(
            num_scalar_prefetch=1,
            grid=(batch,),
            in_specs=[pl.BlockSpec((None, 2 * C_WIDTH, page), (lambda b, pt, j=j: (base + pt[b, j], 0, 0)))
                      for j in range(n_pages)]
            + [pl.BlockSpec((None, seq, C_WIDTH), lambda b, pt: (b, 0, 0)),
               pl.BlockSpec((None, 2 * C_WIDTH, LANES), lambda b, pt: (b, 0, 0))],
            out_specs=pl.BlockSpec((None, seq, C_WIDTH), lambda b, pt: (b, 0, 0)),
            scratch_shapes=[pltpu.VMEM((2 * C_WIDTH, past), F32), pltpu.VMEM((8, C_WIDTH), F32)]),
        out_shape=jax.ShapeDtypeStruct((batch, seq, C_WIDTH), F32),
        compiler_params=_cp("parallel"),
        name="moba_sample",
    )(page_table, *([pages_t] * n_pages), cq.reshape(batch, seq, C_WIDTH), tail)
    return o.reshape(batch * seq, C_WIDTH)


FF_CHUNK = 256


def _ffn_kernel(*refs, seq, tm, d_ff, paged_prev, tiles_per_seq):
    if paged_prev:
        (x_ref, oa_ref, ob_ref, oc_ref, g1_ref, sc2_ref, sh2_ref, g2_ref, n2_ref, wout_ref, wup_ref,
         cw_ref, cb_ref, wdn_ref, p0_ref, p1_ref, y_ref, u_ref, ubuf, acc_scr) = refs
    else:
        (x_ref, oa_ref, ob_ref, oc_ref, g1_ref, sc2_ref, sh2_ref, g2_ref, n2_ref, wout_ref, wup_ref,
         cw_ref, cb_ref, wdn_ref, y_ref, cst_ref, ubuf, acc_scr, carry) = refs
    i = pl.program_id(0)
    mix = (_dot(oa_ref[...], wout_ref[0:A_WIDTH, :])
           + _dot(ob_ref[...], wout_ref[A_WIDTH:A_WIDTH + B_W, :])
           + _dot(oc_ref[...], wout_ref[A_WIDTH + B_W:, :]))
    x1 = x_ref[...] + g1_ref[...] * mix
    h = x1 * lax.rsqrt(jnp.mean(x1 * x1, axis=-1, keepdims=True) + EPS) * n2_ref[...]
    h = (h * (1.0 + sc2_ref[...]) + sh2_ref[...]).astype(BF16)
    acc_scr[...] = jnp.zeros_like(acc_scr)
    if paged_prev:
        trow = _iota((tm, 1), 0) % seq
    else:
        @pl.when(i % tiles_per_seq == 0)
        def _():
            carry[...] = jnp.zeros_like(carry)
    tf = FF_CHUNK
    hdr = 8

    def conv_half(c0):
        c0 = pl.multiple_of(c0, tf)
        u = jnp.dot(h, wup_ref[:, pl.ds(c0, tf)], preferred_element_type=F32)
        ubuf[hdr:hdr + tm, :] = u
        if paged_prev:
            u_ref[:, pl.ds(c0, tf)] = u
            p0 = p0_ref[:, pl.ds(c0, tf)]
            p1 = p1_ref[:, pl.ds(c0, tf)]
            u1 = jnp.where(trow == 0, p1, ubuf[hdr - 1:hdr - 1 + tm, :])
            u2 = jnp.where(trow == 0, p0, jnp.where(trow == 1, p1, ubuf[hdr - 2:hdr - 2 + tm, :]))
        else:
            ubuf[hdr - 2:hdr, :] = carry[:, pl.ds(c0, tf)]
            u1 = ubuf[hdr - 1:hdr - 1 + tm, :]
            u2 = ubuf[hdr - 2:hdr - 2 + tm, :]
            carry[:, pl.ds(c0, tf)] = u[tm - 2:tm, :]
            cst_ref[:, pl.ds(c0, tf)] = u[tm - 2:tm, :]
        w = cw_ref[:, pl.ds(c0, tf)]
        return cb_ref[:, pl.ds(c0, tf)] + w[0:1, :] * u2 + w[1:2, :] * u1 + w[2:3, :] * u

    def chunk(c, carry_):
        f0 = c * tf
        a = conv_half(f0)
        gte = conv_half(d_ff + f0)
        act = (_silu(a) * gte).astype(BF16)
        acc_scr[...] += jnp.dot(act, wdn_ref[pl.ds(pl.multiple_of(f0, tf), tf), :], preferred_element_type=F32)
        return carry_

    lax.fori_loop(0, d_ff // tf, chunk, 0)
    y_ref[...] = x1 + g2_ref[...] * acc_scr[...]


def _ffn(x2, oa, ob, oc, mods, n2, wout_bf, wup_bf, conv_w, conv_b, wdn_bf, prev, *, tm, seq, rows_per_mod):
    n, d = x2.shape
    d_ff = wdn_bf.shape[0]
    assert d_ff % FF_CHUNK == 0 and n % tm == 0
    if rows_per_mod > 1:
        per = rows_per_mod // tm
        mod_spec = pl.BlockSpec((None, 1, d), lambda i: (i // per, 0, 0))
    else:
        mod_spec = pl.BlockSpec((tm, d), lambda i: (i, 0))
    row = lambda wd: pl.BlockSpec((tm, wd), lambda i: (i, 0))
    full = lambda a: pl.BlockSpec(a.shape, lambda i: (0,) * a.ndim)
    in_specs = [row(d), row(A_WIDTH), row(B_W), row(C_WIDTH), mod_spec, mod_spec, mod_spec, mod_spec,
                full(n2), full(wout_bf), full(wup_bf), full(conv_w), full(conv_b), full(wdn_bf)]
    args = [x2, oa, ob, oc, *mods, n2, wout_bf, wup_bf, conv_w, conv_b, wdn_bf]
    scratch = [pltpu.VMEM((tm + 8, FF_CHUNK), F32), pltpu.VMEM((tm, d), F32)]
    if prev is None:
        assert seq % tm == 0 and tm >= 2
        tiles_per_seq = seq // tm
        out_specs = [row(d), pl.BlockSpec((None, CONV_W - 1, 2 * d_ff), lambda i: (i // tiles_per_seq, 0, 0))]
        out_shape = [jax.ShapeDtypeStruct((n, d), F32), jax.ShapeDtypeStruct((n // seq, CONV_W - 1, 2 * d_ff), F32)]
        scratch.append(pltpu.VMEM((CONV_W - 1, 2 * d_ff), F32))
        sem = "arbitrary"
    else:
        assert tm % seq == 0 and seq >= 2
        tiles_per_seq = 1
        in_specs += [row(2 * d_ff), row(2 * d_ff)]
        args += list(prev)
        out_specs = [row(d), row(2 * d_ff)]
        out_shape = [jax.ShapeDtypeStruct((n, d), F32), jax.ShapeDtypeStruct((n, 2 * d_ff), F32)]
        sem = "parallel"
    kern = functools.partial(_ffn_kernel, seq=seq, tm=tm, d_ff=d_ff, paged_prev=prev is not None,
                             tiles_per_seq=tiles_per_seq)
    return pl.pallas_call(
        kern, grid=(n // tm,), in_specs=in_specs, out_specs=out_specs, out_shape=out_shape,
        scratch_shapes=scratch, compiler_params=_cp(sem), name="outproj_convffn",
    )(*args)


def _final_norm_kernel(x_ref, g_ref, o_ref):
    x = x_ref[...]
    o_ref[...] = x * lax.rsqrt(jnp.mean(x * x, axis=-1, keepdims=True) + EPS) * g_ref[...]


def _final_norm(x2, g, tm):
    n, d = x2.shape
    return pl.pallas_call(
        _final_norm_kernel, grid=(n // tm,),
        in_specs=[pl.BlockSpec((tm, d), lambda i: (i, 0)), pl.BlockSpec((1, d), lambda i: (0, 0))],
        out_specs=pl.BlockSpec((tm, d), lambda i: (i, 0)),
        out_shape=jax.ShapeDtypeStruct((n, d), F32),
        compiler_params=_cp("parallel"), name="final_norm",
    )(x2, g)


def _rope_tables(pos):
    half = HEAD_DIM // 2
    inv = ROPE_THETA ** (-jnp.arange(half, dtype=F32) / half)
    ang = pos.astype(F32)[:, None] * inv[None, :]
    cos = jnp.cos(ang)
    sin = jnp.sin(ang)
    return jnp.tile(cos, (1, 4)), jnp.concatenate([-sin, sin, -sin, sin], axis=1), cos.T, sin.T


def _split_w_in(w_in):
    c = 0
    aq = w_in[..., c:c + A_WIDTH]; c += A_WIDTH
    kv = w_in[..., c:c + 3 * KV_W]; c += 3 * KV_W
    gate = w_in[..., c:c + 3 * A_HEADS]; c += 3 * A_HEADS
    hg = w_in[..., c:c + 4 * B_W]; c += 4 * B_W
    cq = w_in[..., c:c + C_WIDTH]; c += C_WIDTH
    ckv = w_in[..., c:c + 2 * C_WIDTH]
    pad = jnp.zeros(w_in.shape[:2] + (GATE_PAD - 3 * A_HEADS,), w_in.dtype)
    w_row = jnp.concatenate([aq, gate, pad, hg, cq], axis=-1).astype(BF16)
    w_kvt = jnp.swapaxes(jnp.concatenate([kv, ckv], axis=-1), 1, 2).astype(BF16)
    return w_row, w_kvt


def _compress_weights(pe, pool, w1, w2, length):
    rep_t = lambda a: jnp.concatenate([a[0], a[0], a[1], a[1]], axis=-1).T
    pet, poolt = rep_t(pe), rep_t(pool)
    pa = jnp.tile(poolt[:, :CMP_STRIDE], (1, length // CMP_STRIDE))
    pb = jnp.tile(poolt[:, CMP_STRIDE:], (1, length // CMP_STRIDE))
    z1 = jnp.zeros_like(w1[0])
    z2 = jnp.zeros_like(w2[0])
    w1bd = jnp.concatenate([jnp.concatenate([w1[0], z1, z1, z1], 1), jnp.concatenate([z1, w1[0], z1, z1], 1),
                            jnp.concatenate([z1, z1, w1[1], z1], 1), jnp.concatenate([z1, z1, z1, w1[1]], 1)], 0)
    w2bd = jnp.concatenate([jnp.concatenate([w2[0], z2, z2, z2], 1), jnp.concatenate([z2, w2[0], z2, z2], 1),
                            jnp.concatenate([z2, z2, w2[1], z2], 1), jnp.concatenate([z2, z2, z2, w2[1]], 1)], 0)
    return pa, pb, pet, poolt, w1bd.T.astype(BF16), w2bd.T.astype(BF16)


def _rows_from_t(xt, heads):
    lead = xt.shape[:-2]
    t = xt.shape[-1]
    x = xt.reshape(lead + (2, heads, HEAD_DIM, t))
    return jnp.moveaxis(x, -1, len(lead))


def kernel(x_prompt, x_sample, cache_nsa_cmp_kv, cache_nsa_sel_kv, cache_nsa_win_kv, state_hgrn, cache_moba_kv, state_ffn_conv, page_table, c_prompt, c_sample, w_ada, b_ada, norm1_g, norm2_g, w_in, w_out, cmp_pe, cmp_pool, cmp_w1, cmp_w2, hgrn_lb_logits, hgrn_norm_g, w_up, conv_w, conv_b, w_down, final_g):
    bp, tp, d = x_prompt.shape
    bs, ts, _ = x_sample.shape
    depth = w_in.shape[0]
    d_ff = w_down.shape[1]
    n_pool, page = cache_moba_kv.shape[1], cache_moba_kv.shape[2]
    past = page_table.shape[1] * page
    np_, ns_ = bp * tp, bs * ts

    w_ada_bf = w_ada.astype(BF16)
    w_row, w_kvt = _split_w_in(w_in)
    w_out_bf = w_out.astype(BF16)
    w_up_bf = w_up.astype(BF16)
    w_dn_bf = w_down.astype(BF16)
    ng4 = jnp.tile(hgrn_norm_g, (1, B_HEADS))

    mod = _modulation(jnp.concatenate([c_prompt, c_sample], axis=0), w_ada_bf, b_ada)
    mod = mod.reshape(depth, bp + bs, 6, d)

    cos_p, sin_p, cost_p, sint_p = _rope_tables(jnp.arange(tp))
    cos_s, sin_s, cost_s, sint_s = _rope_tables(past + jnp.arange(ts))
    cos_s, sin_s = jnp.tile(cos_s, (bs, 1)), jnp.tile(sin_s, (bs, 1))
    cost_s, sint_s = jnp.tile(cost_s, (1, bs)), jnp.tile(sint_s, (1, bs))

    def pages_t(c):
        w = c.shape[3] * c.shape[4] * c.shape[5]
        return jnp.transpose(c, (0, 1, 3, 4, 5, 2)).reshape(c.shape[0] * c.shape[1], w, c.shape[2])

    cmp_pages = pages_t(cache_nsa_cmp_kv)
    sel_pages = pages_t(cache_nsa_sel_kv)
    moba_pages = pages_t(cache_moba_kv)
    win_cache = pages_t(cache_nsa_win_kv)

    def tails(xt):
        c = xt.shape[1]
        x = jnp.transpose(xt.reshape(c, bs, ts), (1, 0, 2))
        return jnp.pad(x, ((0, 0), (0, 0), (LANES - ts, 0)))

    tm_p = min(512, tp)
    tm_f = min(256, tp)
    tm_fs = min(128, ns_)
    xp = x_prompt.reshape(np_, d)
    xs = x_sample.reshape(ns_, d)
    st_p, st_s = [], []
    zero_state = jnp.zeros((bp, B_W, B_W), F32)
    w_keep = min(WINDOW, tp)
    for l in range(depth):
        mp = [mod[l, :bp, j].reshape(bp, 1, d) for j in range(6)]
        ms = [jnp.repeat(mod[l, bp:, j], ts, axis=0) for j in range(6)]
        n1 = norm1_g[l].reshape(1, d)
        n2 = norm2_g[l].reshape(1, d)
        cw_p = _compress_weights(cmp_pe[l], cmp_pool[l], cmp_w1[l], cmp_w2[l], tp)
        cw_s = cw_p if past == tp else _compress_weights(cmp_pe[l], cmp_pool[l], cmp_w1[l], cmp_w2[l], past)

        qa, qar, gate, hg, cq, cmpt, selt, wint, mobat = _in_projection(
            xp, mp[1], mp[0], n1, w_row[l], w_kvt[l], cos_p, sin_p, cost_p, sint_p,
            tm=tm_p, seq=tp, rows_per_mod=tp, rows_per_pos=tp)
        o_a = _nsa_prompt(qa, qar, gate, cmpt, selt, wint, cw_p, batch=bp, seq=tp)
        o_b, st_new = _hgrn(hg, zero_state, hgrn_lb_logits, ng4[l:l + 1], layer=l, batch=bp, seq=tp)
        o_c = _moba_prompt(cq, mobat, batch=bp, seq=tp)
        xp, conv_st = _ffn(xp, o_a, o_b, o_c, (mp[2], mp[4], mp[3], mp[5]), n2, w_out_bf[l], w_up_bf[l],
                           conv_w[l], conv_b[l].reshape(1, -1), w_dn_bf[l], None, tm=tm_f, seq=tp, rows_per_mod=tp)
        st_p.append((cmpt, selt, wint[:, :, tp - w_keep:], _blockdiag_to_state(st_new), mobat, conv_st))

        qa, qar, gate, hg, cq, cmpt, selt, wint, mobat = _in_projection(
            xs, ms[1], ms[0], n1, w_row[l], w_kvt[l], cos_s, sin_s, cost_s, sint_s,
            tm=ns_, seq=ns_, rows_per_mod=1, rows_per_pos=ns_)
        o_a, win_state = _nsa_sample(qa, qar, gate, tails(selt), tails(wint), cmp_pages, sel_pages, win_cache,
                                     page_table, cw_s, layer=l, depth=depth, batch=bs, seq=ts)
        o_b, st_new = _hgrn(hg, _state_to_blockdiag(state_hgrn[l]), hgrn_lb_logits, ng4[l:l + 1],
                            layer=l, batch=bs, seq=ts)
        o_c = _moba_sample(cq, tails(mobat), moba_pages, page_table, layer=l, depth=depth, batch=bs, seq=ts)
        prev = state_ffn_conv[l]
        p0 = jnp.repeat(prev[:, 0], ts, axis=0)
        p1 = jnp.repeat(prev[:, 1], ts, axis=0)
        xs, u_s = _ffn(xs, o_a, o_b, o_c, (ms[2], ms[4], ms[3], ms[5]), n2, w_out_bf[l], w_up_bf[l],
                       conv_w[l], conv_b[l].reshape(1, -1), w_dn_bf[l], (p0, p1), tm=tm_fs, seq=ts, rows_per_mod=1)
        full = jnp.concatenate([prev, u_s.reshape(bs, ts, 2 * d_ff)], axis=1)
        new_t = lambda xt: jnp.transpose(xt.reshape(xt.shape[1], bs, ts), (1, 0, 2))
        st_s.append((new_t(cmpt), new_t(selt), win_state, _blockdiag_to_state(st_new), new_t(mobat),
                     full[:, -(CONV_W - 1):]))

    fg = final_g.reshape(1, d)
    y_p = _final_norm(xp, fg, tm_p).reshape(bp, tp, d)
    y_s = _final_norm(xs, fg, ns_).reshape(bs, ts, d)

    def stacked(sts, j):
        return jnp.stack([st[j] for st in sts], axis=0)

    return (y_p, y_s,
            _rows_from_t(stacked(st_p, 0), A_KV_HEADS), _rows_from_t(stacked(st_p, 1), A_KV_HEADS),
            _rows_from_t(stacked(st_p, 2), A_KV_HEADS), stacked(st_p, 3),
            _rows_from_t(stacked(st_p, 4), C_HEADS), stacked(st_p, 5),
            _rows_from_t(stacked(st_s, 0), A_KV_HEADS), _rows_from_t(stacked(st_s, 1), A_KV_HEADS),
            _rows_from_t(stacked(st_s, 2), A_KV_HEADS), stacked(st_s, 3),
            _rows_from_t(stacked(st_s, 4), C_HEADS), stacked(st_s, 5))
```

```python
import functools

import jax
import jax.numpy as jnp
from jax import lax
from jax.experimental import pallas as pl
from jax.experimental.pallas import tpu as pltpu

F32 = jnp.float32
BF16 = jnp.bfloat16

HEAD_DIM = 64
ROPE_THETA = 10000.0
EPS = 1e-6
NEG = -1e30
A_HEADS = 6
A_KV_HEADS = 2
A_REP = A_HEADS // A_KV_HEADS
CMP_LEN = 32
CMP_STRIDE = 16
CMP_HIDDEN = 128
SEL_BLOCK = 64
SEL_TOP = 16
FORCE_BONUS = 1e4
WINDOW = 512
B_HEADS = 4
B_DK = 64
B_DV = 64
C_HEADS = 6
MOBA_BLOCK = 256
MOBA_TOP = 3
CONV_W = 3

LANES = 128
HG_CHUNK = 16
HG_UNROLL = 4
Q_TILE = 128
VMEM_LIMIT = 56 * 1024 * 1024

A_WIDTH = A_HEADS * HEAD_DIM
KV_W = 2 * A_KV_HEADS * HEAD_DIM
B_W = B_HEADS * B_DK
C_WIDTH = C_HEADS * HEAD_DIM
GATE_PAD = LANES
RO_QA = 0
RO_GATE = RO_QA + A_WIDTH
RO_HG = RO_GATE + GATE_PAD
RO_CQ = RO_HG + 4 * B_W
ROW_W = RO_CQ + C_WIDTH
TO_CMP = 0
TO_SEL = TO_CMP + KV_W
TO_WIN = TO_SEL + KV_W
TO_MK = TO_WIN + KV_W
TO_MV = TO_MK + C_WIDTH
KVT_W = TO_MV + C_WIDTH
ATT_SCALE = HEAD_DIM ** -0.5
assert ATT_SCALE == 0.125
KEY_CHUNK = 512


def _cp(*sem):
    return pltpu.CompilerParams(dimension_semantics=sem, vmem_limit_bytes=VMEM_LIMIT)


def _dot(a, b):
    return jnp.dot(a.astype(BF16), b.astype(BF16), preferred_element_type=F32)


def _dot_nt(a, b):
    return lax.dot_general(a.astype(BF16), b.astype(BF16), (((1,), (1,)), ((), ())),
                           preferred_element_type=F32)


def _dot_tn(a, b):
    return lax.dot_general(a.astype(BF16), b.astype(BF16), (((0,), (0,)), ((), ())),
                           preferred_element_type=F32)


def _split2(x):
    hi = x.astype(BF16)
    lo = (x - hi.astype(F32)).astype(BF16)
    return hi, lo


def _dot_x2(x, w_bf16):
    hi, lo = _split2(x)
    return (jnp.dot(hi, w_bf16, preferred_element_type=F32)
            + jnp.dot(lo, w_bf16, preferred_element_type=F32))


def _sigmoid(x):
    return 1.0 / (1.0 + jnp.exp(-x))


def _silu(x):
    return x * _sigmoid(x)


def _masked_softmax(s, mask):
    s = jnp.where(mask, s, NEG)
    m = jnp.max(s, axis=-1, keepdims=True)
    p = jnp.where(mask, jnp.exp(s - m), 0.0)
    return p / jnp.maximum(jnp.sum(p, axis=-1, keepdims=True), 1e-30)


def _iota(shape, dim):
    return lax.broadcasted_iota(jnp.int32, shape, dim)


def _rope128(x, cos, sin):
    lane = _iota(x.shape, 1)
    first = (lane & (HEAD_DIM - 1)) < (HEAD_DIM // 2)
    swapped = jnp.where(first, pltpu.roll(x, LANES - HEAD_DIM // 2, 1), pltpu.roll(x, HEAD_DIM // 2, 1))
    return x * cos + swapped * sin


def _rank_before(score, n):
    lane = _iota((1, score.shape[1]), 1)
    rank = jnp.zeros(score.shape, F32)
    for j in range(n):
        col = score[:, j:j + 1]
        after_j = jnp.where(lane > j, 1.0, 0.0)
        rank = rank + jnp.where(col > score, 1.0, jnp.where(col == score, after_j, 0.0))
    return rank


def _mod_kernel(c_ref, w_ref, b_ref, o_ref):
    o_ref[...] = _dot(_silu(c_ref[...]), w_ref[...]) + b_ref[...]


def _modulation(c_all, w_ada_bf, b_ada):
    depth, d, n6 = w_ada_bf.shape
    nb = c_all.shape[0]
    tn = n6 // 4
    return pl.pallas_call(
        _mod_kernel,
        grid=(depth, n6 // tn),
        in_specs=[pl.BlockSpec((nb, d), lambda l, j: (0, 0)),
                  pl.BlockSpec((None, d, tn), lambda l, j: (l, 0, j)),
                  pl.BlockSpec((None, 1, tn), lambda l, j: (l, 0, j))],
        out_specs=pl.BlockSpec((None, nb, tn), lambda l, j: (l, 0, j)),
        out_shape=jax.ShapeDtypeStruct((depth, nb, n6), F32),
        compiler_params=_cp("parallel", "parallel"),
        name="adaln_mod",
    )(c_all, w_ada_bf, b_ada.reshape(depth, 1, n6))


def _inproj_kernel(x_ref, sc_ref, sh_ref, g_ref, wrow_ref, wkvt_ref, cos_ref, sin_ref, cost_ref, sint_ref,
                   qa_ref, qar_ref, gate_ref, hg_ref, cq_ref, cmpt_ref, selt_ref, wint_ref, mobat_ref):
    x = x_ref[...]
    h = x * lax.rsqrt(jnp.mean(x * x, axis=-1, keepdims=True) + EPS) * g_ref[...]
    h = (h * (1.0 + sc_ref[...]) + sh_ref[...]).astype(BF16)
    cos = cos_ref[...]
    sin = sin_ref[...]
    cost = cost_ref[...]
    sint = sint_ref[...]

    def proj(c0, n):
        return jnp.dot(h, wrow_ref[:, c0:c0 + n], preferred_element_type=F32)

    def proj_t(r0, n):
        return lax.dot_general(wkvt_ref[r0:r0 + n, :], h, (((1,), (1,)), ((), ())), preferred_element_type=F32)

    def rope_cols(y):
        return jnp.concatenate([_rope128(y[:, j:j + LANES], cos, sin) for j in range(0, y.shape[1], LANES)], axis=1)

    def rope_rows(y):
        hh = HEAD_DIM // 2
        parts = []
        for r0 in range(0, y.shape[0], HEAD_DIM):
            x1 = y[r0:r0 + hh, :]
            x2 = y[r0 + hh:r0 + HEAD_DIM, :]
            parts += [x1 * cost - x2 * sint, x2 * cost + x1 * sint]
        return jnp.concatenate(parts, axis=0)

    qa = proj(RO_QA, A_WIDTH)
    qa_ref[...] = qa
    qar_ref[...] = rope_cols(qa)
    gate_ref[...] = proj(RO_GATE, GATE_PAD)
    hg_ref[...] = proj(RO_HG, 4 * B_W)
    cq_ref[...] = rope_cols(proj(RO_CQ, C_WIDTH))
    half = KV_W // 2
    cmpt_ref[...] = proj_t(TO_CMP, KV_W)
    selt_ref[0:half, :] = rope_rows(proj_t(TO_SEL, half))
    selt_ref[half:, :] = proj_t(TO_SEL + half, half)
    wint_ref[0:half, :] = rope_rows(proj_t(TO_WIN, half))
    wint_ref[half:, :] = proj_t(TO_WIN + half, half)
    mobat_ref[0:C_WIDTH, :] = rope_rows(proj_t(TO_MK, C_WIDTH))
    mobat_ref[C_WIDTH:, :] = proj_t(TO_MV, C_WIDTH)


def _in_projection(x2, sc, sh, gain, w_row, w_kvt, cos, sin, cost, sint, *, tm, seq, rows_per_mod, rows_per_pos):
    n, d = x2.shape
    assert n % seq == 0 and seq % tm == 0 and rows_per_pos % tm == 0
    nseq = n // seq
    tps = seq // tm
    pos_blocks = rows_per_pos // tm
    if rows_per_mod > 1:
        per = rows_per_mod // tm
        mod_spec = pl.BlockSpec((None, 1, d), lambda i: (i // per, 0, 0))
    else:
        mod_spec = pl.BlockSpec((tm, d), lambda i: (i, 0))
    row_w = (A_WIDTH, A_WIDTH, GATE_PAD, 4 * B_W, C_WIDTH)
    t_w = (KV_W, KV_W, KV_W, 2 * C_WIDTH)
    full = lambda a: pl.BlockSpec(a.shape, lambda i: (0,) * a.ndim)
    return pl.pallas_call(
        _inproj_kernel,
        grid=(n // tm,),
        in_specs=[pl.BlockSpec((tm, d), lambda i: (i, 0)), mod_spec, mod_spec, full(gain), full(w_row), full(w_kvt),
                  pl.BlockSpec((tm, LANES), lambda i: (i % pos_blocks, 0)),
                  pl.BlockSpec((tm, LANES), lambda i: (i % pos_blocks, 0)),
                  pl.BlockSpec((HEAD_DIM // 2, tm), lambda i: (0, i % pos_blocks)),
                  pl.BlockSpec((HEAD_DIM // 2, tm), lambda i: (0, i % pos_blocks))],
        out_specs=[pl.BlockSpec((tm, wd), lambda i: (i, 0)) for wd in row_w]
        + [pl.BlockSpec((None, wd, tm), lambda i: (i // tps, 0, i % tps)) for wd in t_w],
        out_shape=[jax.ShapeDtypeStruct((n, wd), F32) for wd in row_w]
        + [jax.ShapeDtypeStruct((nseq, wd, seq), F32) for wd in t_w],
        compiler_params=_cp("parallel"),
        name="in_projection",
    )(x2, sc, sh, gain, w_row, w_kvt, cos, sin, cost, sint)


def _hgrn_kernel(hg_ref, st0_ref, lbl_ref, ng_ref, o_ref, st_out_ref,
                 st_scr, b_scr, key_scr, q_scr, v_scr, g_scr, o_scr, *, layer, valid, rows, nt):
    t = pl.program_id(1)

    @pl.when(t == 0)
    def _():
        st_scr[...] = st0_ref[...]

    lg = lbl_ref[...]
    e = jnp.exp(lg - jnp.max(lg, axis=0, keepdims=True))
    p = e / jnp.sum(e, axis=0, keepdims=True)
    acc = p[0:1, :]
    for i in range(1, layer + 1):
        acc = acc + p[i:i + 1, :]
    lb = jnp.clip(acc - p[0:1, :], 0.0, 1.0)
    lbpos = lb > 0.0

    if valid < rows:
        g_scr[...] = jnp.zeros_like(g_scr)
        v_scr[...] = jnp.zeros_like(v_scr)
        q_scr[...] = jnp.zeros_like(q_scr)
        key_scr[...] = jnp.zeros_like(key_scr)
        q_scr[0:valid, :] = hg_ref[:, 0:B_W]
        key_scr[0:valid, :] = hg_ref[:, B_W:2 * B_W]
        v_scr[0:valid, :] = hg_ref[:, 2 * B_W:3 * B_W]
        g_scr[0:valid, :] = hg_ref[:, 3 * B_W:4 * B_W]
        q = q_scr[...]
        z = key_scr[...]
    else:
        q = hg_ref[:, 0:B_W]
        z = hg_ref[:, B_W:2 * B_W]
        v_scr[...] = hg_ref[:, 2 * B_W:3 * B_W]
        g_scr[...] = hg_ref[:, 3 * B_W:4 * B_W]

    ls = jnp.minimum(z, 0.0) - jnp.log1p(jnp.exp(-jnp.abs(z)))
    c_ = jnp.log1p(-lb) + ls
    a_ = jnp.log(jnp.where(lbpos, lb, 1.0))
    logf = jnp.where(lbpos, jnp.maximum(a_, c_) + jnp.log1p(jnp.exp(-jnp.abs(a_ - c_))), c_)
    key = (1.0 - lb) * _sigmoid(-z)
    rowi = _iota((rows, 1), 0)
    live = rowi < valid
    logf = jnp.where(live, logf, 0.0)
    key = jnp.where(live, key, 0.0)
    ri = _iota((rows, rows), 0)
    ci = _iota((rows, rows), 1)
    tri = jnp.where(((ri // HG_CHUNK) == (ci // HG_CHUNK)) & (ci <= ri), 1.0, 0.0).astype(BF16)
    h1 = logf.astype(BF16)
    r1 = logf - h1.astype(F32)
    h2 = r1.astype(BF16)
    h3 = (r1 - h2.astype(F32)).astype(BF16)
    b_scr[...] = (jnp.dot(tri, h1, preferred_element_type=F32) + jnp.dot(tri, h2, preferred_element_type=F32)
                  + jnp.dot(tri, h3, preferred_element_type=F32))
    key_scr[...] = key
    q_scr[...] = _silu(q)

    hr = _iota((B_W, B_W), 0)
    hc = _iota((B_W, B_W), 1)
    same_head = (hr // B_DK) == (hc // B_DK)
    head_ones = jnp.where(same_head, 1.0, 0.0).astype(BF16)
    ti = _iota((HG_CHUNK, 1), 0)
    ng = ng_ref[...]

    def chunk(c, carry):
        r0 = pl.multiple_of(c * HG_CHUNK, HG_CHUNK)
        b = b_scr[pl.ds(r0, HG_CHUNK), :]
        kk = key_scr[pl.ds(r0, HG_CHUNK), :]
        qf = q_scr[pl.ds(r0, HG_CHUNK), :]
        v = v_scr[pl.ds(r0, HG_CHUNK), :]
        gt = g_scr[pl.ds(r0, HG_CHUNK), :]
        st = st_scr[...]
        blast = b[HG_CHUNK - 1:HG_CHUNK, :]
        o_inter = _dot_nt(qf * jnp.exp(b), st)
        es = []
        for s in range(HG_CHUNK):
            e_s = qf * kk[s:s + 1, :] * jnp.exp(jnp.minimum(b - b[s:s + 1, :], 0.0))
            es.append(jnp.where(ti >= s, e_s, 0.0))
        r = _dot_x2(jnp.concatenate(es, axis=0), head_ones)
        o = o_inter
        for s in range(HG_CHUNK):
            o = o + r[s * HG_CHUNK:(s + 1) * HG_CHUNK, :] * v[s:s + 1, :]
        ms = _dot_x2(o * o, head_ones) * (1.0 / B_DV)
        o_scr[pl.ds(r0, HG_CHUNK), :] = o * lax.rsqrt(ms + EPS) * ng * _silu(gt)
        upd = _dot_tn(v, kk * jnp.exp(blast - b))
        st_scr[...] = st * jnp.exp(blast) + jnp.where(same_head, upd, 0.0)
        return carry

    n_chunks = rows // HG_CHUNK
    lax.fori_loop(0, n_chunks, chunk, 0, unroll=min(HG_UNROLL, n_chunks))
    o_ref[...] = o_scr[0:valid, :]

    @pl.when(t == nt - 1)
    def _():
        st_out_ref[...] = st_scr[...]


def _hgrn(hg, st0, lb_logits, norm_g4, *, layer, batch, seq):
    if seq % HG_CHUNK == 0:
        tt = min(seq, 256)
        assert seq % tt == 0
        nt = seq // tt
        rows = valid = tt
        hg_in = hg
        hg_spec = pl.BlockSpec((tt, 4 * B_W), lambda b, t: (b * nt + t, 0))
        o_spec = pl.BlockSpec((tt, B_W), lambda b, t: (b * nt + t, 0))
        o_shape = jax.ShapeDtypeStruct((batch * seq, B_W), F32)
    else:
        assert seq < HG_CHUNK
        nt, rows, valid = 1, HG_CHUNK, seq
        hg_in = hg.reshape(batch, seq, 4 * B_W)
        hg_spec = pl.BlockSpec((None, seq, 4 * B_W), lambda b, t: (b, 0, 0))
        o_spec = pl.BlockSpec((None, seq, B_W), lambda b, t: (b, 0, 0))
        o_shape = jax.ShapeDtypeStruct((batch, seq, B_W), F32)
    depth = lb_logits.shape[0]
    kern = functools.partial(_hgrn_kernel, layer=layer, valid=valid, rows=rows, nt=nt)
    o, st = pl.pallas_call(
        kern,
        grid=(batch, nt),
        in_specs=[hg_spec,
                  pl.BlockSpec((None, B_W, B_W), lambda b, t: (b, 0, 0)),
                  pl.BlockSpec((depth, B_W), lambda b, t: (0, 0)),
                  pl.BlockSpec((1, B_W), lambda b, t: (0, 0))],
        out_specs=[o_spec, pl.BlockSpec((None, B_W, B_W), lambda b, t: (b, 0, 0))],
        out_shape=[o_shape, jax.ShapeDtypeStruct((batch, B_W, B_W), F32)],
        scratch_shapes=[pltpu.VMEM((B_W, B_W), F32)] + [pltpu.VMEM((rows, B_W), F32)] * 6,
        compiler_params=_cp("parallel", "arbitrary"),
        name="hgrn2",
    )(hg_in, st0, lb_logits, norm_g4)
    return o.reshape(batch * seq, B_W), st


def _state_to_blockdiag(s0):
    s0t = jnp.swapaxes(s0, 2, 3)
    z = jnp.zeros_like(s0t[:, 0])
    rows = [jnp.concatenate([s0t[:, h] if g == h else z for g in range(B_HEADS)], axis=-1) for h in range(B_HEADS)]
    return jnp.concatenate(rows, axis=1)


def _blockdiag_to_state(st):
    b = st.shape[0]
    st5 = st.reshape(b, B_HEADS, B_DV, B_HEADS, B_DK)
    diag = jnp.stack([st5[:, h, :, h, :] for h in range(B_HEADS)], axis=1)
    return jnp.swapaxes(diag, 2, 3)


def _tile_rows(x, n):
    return jnp.concatenate([x] * n, axis=0)


def _softmax_pv(q, kt, vt, bias):
    s = _dot(q * ATT_SCALE, kt) + bias
    p = jnp.exp(s - jnp.max(s, axis=-1, keepdims=True))
    return _dot_nt(p, vt) / jnp.sum(p, axis=-1, keepdims=True)


def _softmax_pv2(q, kt1, vt1, bias1, kt2, vt2, bias2):
    qs = q * ATT_SCALE
    s1 = _dot(qs, kt1) + bias1
    s2 = _dot(qs, kt2) + bias2
    m = jnp.maximum(jnp.max(s1, axis=-1, keepdims=True), jnp.max(s2, axis=-1, keepdims=True))
    p1 = jnp.exp(s1 - m)
    p2 = jnp.exp(s2 - m)
    den = jnp.sum(p1, axis=-1, keepdims=True) + jnp.sum(p2, axis=-1, keepdims=True)
    return (_dot_nt(p1, vt1) + _dot_nt(p2, vt2)) / den


def _compress_t(xt, pa, pb, pet, poolt, w1t, w2t, nblk):
    length = xt.shape[1]
    tb = _iota((length, nblk), 0) // CMP_STRIDE
    nb = _iota((length, nblk), 1)
    first = jnp.where(tb == nb, 1.0, 0.0).astype(BF16)
    second = jnp.where(tb == nb + 1, 1.0, 0.0).astype(BF16)
    bias = jnp.sum(pet * poolt, axis=1, keepdims=True)
    pooled = _dot(xt * pa, first) + _dot(xt * pb, second) + bias
    hid = _silu(jnp.dot(w1t, pooled.astype(BF16), preferred_element_type=F32))
    return jnp.dot(w2t, hid.astype(BF16), preferred_element_type=F32)


def _nsa_select(qa, cmpt, qpos, *, n_cmp, n_sel):
    tq = qa.shape[0]
    ncp = cmpt.shape[1]
    n_i = _iota((1, ncp), 1)
    maskf = jnp.where((n_i * CMP_STRIDE + (CMP_LEN - 1) <= qpos) & (n_i < n_cmp), 1.0, 0.0)
    mask3 = _tile_rows(maskf, A_REP) > 0.5
    on = _iota((ncp, LANES), 0) * CMP_STRIDE
    os_ = _iota((ncp, LANES), 1) * SEL_BLOCK
    overlap = jnp.where((on < os_ + SEL_BLOCK) & (on + CMP_LEN > os_), 1.0, 0.0).astype(BF16)
    s_i = _iota((1, LANES), 1)
    cur = qpos // SEL_BLOCK
    forced = (s_i == 0) | (s_i == cur) | (s_i == cur - 1)
    visible = (s_i * SEL_BLOCK <= qpos) & (s_i < n_sel)
    half = KV_W // 2
    o_cmp, chosen = [], []
    for g in range(A_KV_HEADS):
        kct = cmpt[g * HEAD_DIM:(g + 1) * HEAD_DIM, :]
        vct = cmpt[half + g * HEAD_DIM:half + (g + 1) * HEAD_DIM, :]
        q3 = _stack_heads(qa, g)
        p = _masked_softmax(_dot(q3, kct) * ATT_SCALE, mask3)
        o_cmp.append(_dot_nt(p, vct))
        p_sum = p[0:tq]
        for r in range(1, A_REP):
            p_sum = p_sum + p[r * tq:(r + 1) * tq]
        imp = _dot_x2(p_sum, overlap)
        score = jnp.where(visible, imp + jnp.where(forced, FORCE_BONUS, 0.0), NEG)
        chosen.append(jnp.where(_rank_before(score, n_sel) < min(SEL_TOP, n_sel), 1.0, 0.0))
    return o_cmp, chosen


def _stack_heads(q, g):
    return jnp.concatenate([q[:, (g * A_REP + r) * HEAD_DIM:(g * A_REP + r + 1) * HEAD_DIM] for r in range(A_REP)], axis=0)


def _compress_kernel(xt_ref, pa_ref, pb_ref, pet_ref, poolt_ref, w1t_ref, w2t_ref, o_ref, *, nblk):
    o_ref[...] = _compress_t(xt_ref[...], pa_ref[...], pb_ref[...], pet_ref[...], poolt_ref[...],
                             w1t_ref[...], w2t_ref[...], nblk)


def _nsa_prompt_kernel(qa_ref, qar_ref, gate_ref, cmp_ref, selt_ref, wint_ref, o_ref, part_scr, ch_scr,
                       *, seq, tq, n_cmp, n_sel, span, kchunk):
    q0 = pl.program_id(1) * tq
    qpos = q0 + _iota((tq, 1), 0)
    gs = _sigmoid(gate_ref[...])
    qar = qar_ref[...]
    o_cmp, chosen = _nsa_select(qa_ref[...], cmp_ref[...], qpos, n_cmp=n_cmp, n_sel=n_sel)
    start = pl.multiple_of(jnp.clip(q0 - WINDOW, 0, seq - span), LANES)
    kpos_w = start + _iota((1, span), 1)
    wbias = _tile_rows(jnp.where(kpos_w <= qpos, jnp.where(qpos - kpos_w <= WINDOW, 0.0, NEG), NEG), A_REP)
    half = KV_W // 2
    parts = []
    for g in range(A_KV_HEADS):
        q3 = _stack_heads(qar, g)
        o_win = _softmax_pv(q3, wint_ref[g * HEAD_DIM:(g + 1) * HEAD_DIM, pl.ds(start, span)],
                            wint_ref[half + g * HEAD_DIM:half + (g + 1) * HEAD_DIM, pl.ds(start, span)], wbias)
        for r in range(A_REP):
            c0 = 3 * (g * A_REP + r)
            parts.append(gs[:, c0:c0 + 1] * o_cmp[g][r * tq:(r + 1) * tq] + gs[:, c0 + 2:c0 + 3] * o_win[r * tq:(r + 1) * tq])
        ch_scr[g] = chosen[g]
    part_scr[...] = jnp.concatenate(parts, axis=1)
    bucket = (q0 + tq - 1) // kchunk
    for k in range(seq // kchunk):
        @pl.when(bucket == k)
        def _(k=k):
            klen = (k + 1) * kchunk
            expand = jnp.where((_iota((LANES, klen), 1) // SEL_BLOCK) == _iota((LANES, klen), 0), 1.0, 0.0).astype(BF16)
            causal_bias = jnp.where(_iota((1, klen), 1) <= qpos, 0.0, NEG)
            outs = []
            for g in range(A_KV_HEADS):
                ck = jnp.dot(ch_scr[g].astype(BF16), expand, preferred_element_type=F32)
                bias = _tile_rows(jnp.where(ck > 0.5, causal_bias, NEG), A_REP)
                o_sel = _softmax_pv(_stack_heads(qar, g), selt_ref[g * HEAD_DIM:(g + 1) * HEAD_DIM, 0:klen],
                                    selt_ref[half + g * HEAD_DIM:half + (g + 1) * HEAD_DIM, 0:klen], bias)
                for r in range(A_REP):
                    c0 = 3 * (g * A_REP + r)
                    outs.append(gs[:, c0 + 1:c0 + 2] * o_sel[r * tq:(r + 1) * tq])
            o_ref[...] = part_scr[...] + jnp.concatenate(outs, axis=1)


def _nsa_prompt(qa, qar, gate, cmpt, selt, wint, cw, *, batch, seq):
    pa, pb, pet, poolt, w1t, w2t = cw
    assert seq % Q_TILE == 0 and seq % CMP_STRIDE == 0 and seq >= CMP_LEN
    nblk = seq // CMP_STRIDE
    n_cmp = (seq - CMP_LEN) // CMP_STRIDE + 1
    n_sel = -(-seq // SEL_BLOCK)
    assert nblk <= LANES and n_sel <= LANES and n_cmp == nblk - 1
    tq = Q_TILE
    nq = seq // tq
    span = min(seq, tq + WINDOW)
    kchunk = min(KEY_CHUNK, seq)
    assert seq % kchunk == 0 and kchunk % tq == 0
    full = lambda a: pl.BlockSpec(a.shape, lambda b: (0,) * a.ndim)
    cmpkv = pl.pallas_call(
        functools.partial(_compress_kernel, nblk=nblk),
        grid=(batch,),
        in_specs=[pl.BlockSpec((None, KV_W, seq), lambda b: (b, 0, 0)),
                  full(pa), full(pb), full(pet), full(poolt), full(w1t), full(w2t)],
        out_specs=pl.BlockSpec((None, KV_W, nblk), lambda b: (b, 0, 0)),
        out_shape=jax.ShapeDtypeStruct((batch, KV_W, nblk), F32),
        compiler_params=_cp("parallel"),
        name="nsa_compress",
    )(cmpt, pa, pb, pet, poolt, w1t, w2t)
    kern = functools.partial(_nsa_prompt_kernel, seq=seq, tq=tq, n_cmp=n_cmp, n_sel=n_sel, span=span, kchunk=kchunk)
    return pl.pallas_call(
        kern,
        grid=(batch, nq),
        in_specs=[pl.BlockSpec((tq, A_WIDTH), lambda b, i: (b * nq + i, 0)),
                  pl.BlockSpec((tq, A_WIDTH), lambda b, i: (b * nq + i, 0)),
                  pl.BlockSpec((tq, GATE_PAD), lambda b, i: (b * nq + i, 0)),
                  pl.BlockSpec((None, KV_W, nblk), lambda b, i: (b, 0, 0)),
                  pl.BlockSpec((None, KV_W, seq), lambda b, i: (b, 0, 0)),
                  pl.BlockSpec((None, KV_W, seq), lambda b, i: (b, 0, 0))],
        out_specs=pl.BlockSpec((tq, A_WIDTH), lambda b, i: (b * nq + i, 0)),
        out_shape=jax.ShapeDtypeStruct((batch * seq, A_WIDTH), F32),
        scratch_shapes=[pltpu.VMEM((tq, A_WIDTH), F32), pltpu.VMEM((A_KV_HEADS, tq, LANES), F32)],
        compiler_params=_cp("parallel", "parallel"),
        name="nsa_prompt",
    )(qa, qar, gate, cmpkv, selt, wint)


def _nsa_sample_kernel(pt_ref, *refs, n_pages, page, seq, past, wb):
    cmp_pages = refs[:n_pages]
    sel_pages = refs[n_pages:2 * n_pages]
    (wcache_ref, qa_ref, qar_ref, gate_ref, stail_ref, wtail_ref, pa_ref, pb_ref, pet_ref, poolt_ref, w1t_ref, w2t_ref,
     o_ref, wstate_ref, cmp_scr, sel_scr, q_scr) = refs[2 * n_pages:]
    del pt_ref
    for j in range(n_pages):
        cmp_scr[:, j * page:(j + 1) * page] = cmp_pages[j][...]
        sel_scr[:, j * page:(j + 1) * page] = sel_pages[j][...]
    tq = q_scr.shape[1]
    q_scr[...] = jnp.zeros_like(q_scr)
    q_scr[0, 0:seq, :] = qa_ref[...]
    q_scr[1, 0:seq, :] = qar_ref[...]
    q_scr[2, 0:seq, 0:GATE_PAD] = gate_ref[...]
    nblk = past // CMP_STRIDE
    total = past + seq
    n_cmp = (total - CMP_LEN) // CMP_STRIDE + 1
    n_sel = -(-total // SEL_BLOCK)
    new_blk = past // SEL_BLOCK
    cmpt = _compress_t(cmp_scr[...], pa_ref[...], pb_ref[...], pet_ref[...], poolt_ref[...],
                       w1t_ref[...], w2t_ref[...], nblk)
    half = KV_W // 2
    zero = jnp.zeros((tq, HEAD_DIM), F32)

    def stack_bd(q):
        blocks = []
        for h in range(A_HEADS):
            qh = q[:, h * HEAD_DIM:(h + 1) * HEAD_DIM]
            blocks.append(jnp.concatenate([qh, zero] if h < A_REP else [zero, qh], axis=1))
        return jnp.concatenate(blocks, axis=0)

    def per_group(x):
        return jnp.concatenate([x[0:tq]] * A_REP + [x[tq:2 * tq]] * A_REP, axis=0)

    rows = A_HEADS * tq
    trow = _iota((rows, 1), 0) % tq
    qpos = past + trow
    qa_bd = stack_bd(q_scr[0])
    qar_bd = stack_bd(q_scr[1])
    gs = _sigmoid(q_scr[2][:, 0:GATE_PAD])
    n_i = _iota((1, nblk), 1)
    cmp_mask = (n_i * CMP_STRIDE + (CMP_LEN - 1) <= qpos) & (n_i < n_cmp)
    p = _masked_softmax(_dot(qa_bd, cmpt[0:half, :]) * ATT_SCALE, cmp_mask)
    o_cmp = _dot_nt(p, cmpt[half:, :])
    p_sum = jnp.concatenate([p[g * A_REP * tq:g * A_REP * tq + tq] + p[g * A_REP * tq + tq:g * A_REP * tq + 2 * tq]
                             + p[g * A_REP * tq + 2 * tq:(g + 1) * A_REP * tq] for g in range(A_KV_HEADS)], axis=0)
    qpos_g = past + _iota((A_KV_HEADS * tq, 1), 0) % tq
    on = _iota((nblk, LANES), 0) * CMP_STRIDE
    os_ = _iota((nblk, LANES), 1) * SEL_BLOCK
    overlap = jnp.where((on < os_ + SEL_BLOCK) & (on + CMP_LEN > os_), 1.0, 0.0).astype(BF16)
    s_i = _iota((1, LANES), 1)
    cur = qpos_g // SEL_BLOCK
    forced = (s_i == 0) | (s_i == cur) | (s_i == cur - 1)
    visible = (s_i * SEL_BLOCK <= qpos_g) & (s_i < n_sel)
    score = jnp.where(visible, _dot_x2(p_sum, overlap) + jnp.where(forced, FORCE_BONUS, 0.0), NEG)
    chosen = jnp.where(_rank_before(score, n_sel) < min(SEL_TOP, n_sel), 1.0, 0.0)
    expand = jnp.where((_iota((LANES, past), 1) // SEL_BLOCK) == _iota((LANES, past), 0), 1.0, 0.0).astype(BF16)
    ck = jnp.dot(chosen.astype(BF16), expand, preferred_element_type=F32)
    sb1 = per_group(jnp.where(ck > 0.5, 0.0, NEG))
    tnew = _iota((1, LANES), 1) - (LANES - seq)
    newok = (tnew >= 0) & (tnew <= trow) & (trow - tnew <= WINDOW)
    new_sel = per_group(chosen[:, new_blk:new_blk + 1]) > 0.5
    sb2 = jnp.where(newok & new_sel, 0.0, NEG)
    o_sel = _softmax_pv2(qar_bd, sel_scr[0:half, :], sel_scr[half:, :], sb1, stail_ref[0:half, :], stail_ref[half:, :], sb2)
    kpos_w = past - wb + _iota((1, wb), 1)
    wb1 = jnp.where((kpos_w <= qpos) & (qpos - kpos_w <= WINDOW), 0.0, NEG)
    wb2 = jnp.where(newok, 0.0, NEG)
    o_win = _softmax_pv2(qar_bd, wcache_ref[0:half, :], wcache_ref[half:, :], wb1,
                         wtail_ref[0:half, :], wtail_ref[half:, :], wb2)
    outs = []
    for h in range(A_HEADS):
        g = h // A_REP
        rs = slice(h * tq, (h + 1) * tq)
        ls = slice(g * HEAD_DIM, (g + 1) * HEAD_DIM)
        c0 = 3 * h
        outs.append(gs[:, c0:c0 + 1] * o_cmp[rs, ls] + gs[:, c0 + 1:c0 + 2] * o_sel[rs, ls] + gs[:, c0 + 2:c0 + 3] * o_win[rs, ls])
    o_ref[...] = jnp.concatenate(outs, axis=1)[0:seq, :]
    rolled = pltpu.roll(wcache_ref[...], wb - seq, 1)
    wstate_ref[:, 0:wb - LANES] = rolled[:, 0:wb - LANES]
    lane = _iota((1, LANES), 1)
    wstate_ref[:, wb - LANES:wb] = jnp.where(lane >= LANES - seq, wtail_ref[...], rolled[:, wb - LANES:wb])


def _nsa_sample(qa, qar, gate, stail, wtail, cmp_pages_t, sel_pages_t, win_cache_t, page_table, cw,
                *, layer, depth, batch, seq):
    pa, pb, pet, poolt, w1t, w2t = cw
    n_pages = page_table.shape[1]
    page = cmp_pages_t.shape[2]
    n_pool = cmp_pages_t.shape[0] // depth
    past = n_pages * page
    wb = win_cache_t.shape[2]
    total = past + seq
    nblk = past // CMP_STRIDE
    assert past % CMP_STRIDE == 0 and seq < CMP_STRIDE and past >= CMP_LEN and seq <= 8
    assert (total - CMP_LEN) // CMP_STRIDE + 1 == nblk - 1 and nblk <= LANES
    assert -(-total // SEL_BLOCK) <= LANES and past % SEL_BLOCK == 0 and page % LANES == 0
    assert wb == WINDOW and wb % LANES == 0 and wb > LANES
    base = layer * n_pool

    def page_spec(j):
        return pl.BlockSpec((None, KV_W, page), lambda b, pt: (base + pt[b, j], 0, 0))

    def tok_spec(wd):
        return pl.BlockSpec((None, seq, wd), lambda b, pt: (b, 0, 0))

    def full_spec(a):
        return pl.BlockSpec(a.shape, lambda b, pt: (0,) * a.ndim)

    tail_spec = pl.BlockSpec((None, KV_W, LANES), lambda b, pt: (b, 0, 0))
    kern = functools.partial(_nsa_sample_kernel, n_pages=n_pages, page=page, seq=seq, past=past, wb=wb)
    r3 = lambda a: a.reshape(batch, seq, a.shape[-1])
    o, wstate = pl.pallas_call(
        kern,
        grid_spec=pltpu.PrefetchScalarGridSpec(
            num_scalar_prefetch=1,
            grid=(batch,),
            in_specs=[page_spec(j) for j in range(n_pages)] + [page_spec(j) for j in range(n_pages)]
            + [pl.BlockSpec((None, KV_W, wb), lambda b, pt: (layer * batch + b, 0, 0)),
               tok_spec(A_WIDTH), tok_spec(A_WIDTH), tok_spec(GATE_PAD), tail_spec, tail_spec,
               full_spec(pa), full_spec(pb), full_spec(pet), full_spec(poolt), full_spec(w1t), full_spec(w2t)],
            out_specs=[tok_spec(A_WIDTH), pl.BlockSpec((None, KV_W, wb), lambda b, pt: (b, 0, 0))],
            scratch_shapes=[pltpu.VMEM((KV_W, past), F32), pltpu.VMEM((KV_W, past), F32),
                            pltpu.VMEM((3, 8, A_WIDTH), F32)]),
        out_shape=[jax.ShapeDtypeStruct((batch, seq, A_WIDTH), F32),
                   jax.ShapeDtypeStruct((batch, KV_W, wb), F32)],
        compiler_params=_cp("parallel"),
        name="nsa_sample",
    )(page_table, *([cmp_pages_t] * n_pages), *([sel_pages_t] * n_pages), win_cache_t,
      r3(qa), r3(qar), r3(gate), stail, wtail, pa, pb, pet, poolt, w1t, w2t)
    return o.reshape(batch * seq, A_WIDTH), wstate


def _moba_means_t(kt, n_keys):
    length = kt.shape[1]
    tb = _iota((length, LANES), 0) // MOBA_BLOCK
    nb = _iota((length, LANES), 1)
    bm = jnp.where((tb == nb) & (_iota((length, LANES), 0) < n_keys), 1.0 / MOBA_BLOCK, 0.0).astype(BF16)
    return _dot_x2(kt, bm)


def _moba_choose(qh, mt, past_ok, n_blocks):
    q_hi, q_lo = _split2(qh)
    m_hi, m_lo = _split2(mt)
    gate = (jnp.dot(q_hi, m_hi, preferred_element_type=F32) + jnp.dot(q_lo, m_hi, preferred_element_type=F32)
            + jnp.dot(q_hi, m_lo, preferred_element_type=F32))
    gate = jnp.where(past_ok, gate, NEG)
    return jnp.where(past_ok & (_rank_before(gate, n_blocks) < min(MOBA_TOP, n_blocks)), 1.0, 0.0)


def _moba_prompt_kernel(q_ref, kvt_ref, o_ref, means_scr, ch_scr, *, seq, tq, n_blocks, kchunk):
    @pl.when(pl.program_id(1) == 0)
    def _():
        means_scr[...] = _moba_means_t(kvt_ref[0:C_WIDTH, :], seq)

    q0 = pl.program_id(1) * tq
    qpos = q0 + _iota((tq, 1), 0)
    q = q_ref[...]
    q_blk = qpos // MOBA_BLOCK
    past_ok = _iota((1, LANES), 1) < q_blk
    for h in range(C_HEADS):
        hs = slice(h * HEAD_DIM, (h + 1) * HEAD_DIM)
        ch_scr[h] = _moba_choose(q[:, hs], means_scr[hs, :], past_ok, n_blocks)
    bucket = (q0 + tq - 1) // kchunk
    for k in range(seq // kchunk):
        @pl.when(bucket == k)
        def _(k=k):
            klen = (k + 1) * kchunk
            expand = jnp.where((_iota((LANES, klen), 1) // MOBA_BLOCK) == _iota((LANES, klen), 0), 1.0, 0.0).astype(BF16)
            kpos = _iota((1, klen), 1)
            own_bias = jnp.where((kpos // MOBA_BLOCK) == q_blk, jnp.where(kpos <= qpos, 0.0, NEG), NEG)
            outs = []
            for h in range(C_HEADS):
                hs = slice(h * HEAD_DIM, (h + 1) * HEAD_DIM)
                vs = slice(C_WIDTH + h * HEAD_DIM, C_WIDTH + (h + 1) * HEAD_DIM)
                ck = jnp.dot(ch_scr[h].astype(BF16), expand, preferred_element_type=F32)
                bias = jnp.where(ck > 0.5, 0.0, own_bias)
                outs.append(_softmax_pv(q[:, hs], kvt_ref[hs, 0:klen], kvt_ref[vs, 0:klen], bias))
            o_ref[...] = jnp.concatenate(outs, axis=1)


def _moba_prompt(cq, kvt, *, batch, seq):
    tq = Q_TILE
    nq = seq // tq
    n_blocks = -(-seq // MOBA_BLOCK)
    kchunk = min(KEY_CHUNK, seq)
    assert seq % tq == 0 and n_blocks <= LANES and seq % kchunk == 0 and kchunk % tq == 0
    kern = functools.partial(_moba_prompt_kernel, seq=seq, tq=tq, n_blocks=n_blocks, kchunk=kchunk)
    return pl.pallas_call(
        kern,
        grid=(batch, nq),
        in_specs=[pl.BlockSpec((tq, C_WIDTH), lambda b, i: (b * nq + i, 0)),
                  pl.BlockSpec((None, 2 * C_WIDTH, seq), lambda b, i: (b, 0, 0))],
        out_specs=pl.BlockSpec((tq, C_WIDTH), lambda b, i: (b * nq + i, 0)),
        out_shape=jax.ShapeDtypeStruct((batch * seq, C_WIDTH), F32),
        scratch_shapes=[pltpu.VMEM((C_WIDTH, LANES), F32), pltpu.VMEM((C_HEADS, tq, LANES), F32)],
        compiler_params=_cp("parallel", "arbitrary"),
        name="moba_prompt",
    )(cq, kvt)


def _moba_sample_kernel(pt_ref, *refs, n_pages, page, seq, past):
    pages = refs[:n_pages]
    q_ref, tail_ref, o_ref, kv_scr, q_scr = refs[n_pages:]
    del pt_ref
    for j in range(n_pages):
        kv_scr[:, j * page:(j + 1) * page] = pages[j][...]
    tq = q_scr.shape[0]
    q_scr[...] = jnp.zeros_like(q_scr)
    q_scr[0:seq, :] = q_ref[...]
    rows = C_HEADS * tq
    q_bd = _tile_rows(q_scr[...], C_HEADS)
    diag = (_iota((rows, C_WIDTH), 0) // tq) == (_iota((rows, C_WIDTH), 1) // HEAD_DIM)
    q_bd = jnp.where(diag, q_bd, 0.0)
    n_blocks = -(-(past + seq) // MOBA_BLOCK)
    means = _moba_means_t(kv_scr[0:C_WIDTH, :], past)
    trow = _iota((rows, 1), 0) % tq
    qpos = past + trow
    past_ok = _iota((1, LANES), 1) < (qpos // MOBA_BLOCK)
    expand = jnp.where((_iota((LANES, past), 1) // MOBA_BLOCK) == _iota((LANES, past), 0), 1.0, 0.0).astype(BF16)
    tnew = _iota((1, LANES), 1) - (LANES - seq)
    b2 = jnp.where((tnew >= 0) & (tnew <= trow), 0.0, NEG)
    ch = _moba_choose(q_bd, means, past_ok, n_blocks)
    b1 = jnp.where(jnp.dot(ch.astype(BF16), expand, preferred_element_type=F32) > 0.5, 0.0, NEG)
    o_all = _softmax_pv2(q_bd, kv_scr[0:C_WIDTH, :], kv_scr[C_WIDTH:, :], b1,
                         tail_ref[0:C_WIDTH, :], tail_ref[C_WIDTH:, :], b2)
    o_all = jnp.where(diag, o_all, 0.0)
    o = o_all[0:tq]
    for h in range(1, C_HEADS):
        o = o + o_all[h * tq:(h + 1) * tq]
    o_ref[...] = o[0:seq, :]


def _moba_sample(cq, tail, pages_t, page_table, *, layer, depth, batch, seq):
    n_pages = page_table.shape[1]
    page = pages_t.shape[2]
    n_pool = pages_t.shape[0] // depth
    past = n_pages * page
    assert past % MOBA_BLOCK == 0 and seq <= 8 and -(-(past + seq) // MOBA_BLOCK) <= LANES and page % LANES == 0
    base = layer * n_pool
    kern = functools.partial(_moba_sample_kernel, n_pages=n_pages, page=page, seq=seq, past=past)
    o = pl.pallas_call(
        kern,
        grid_spec=pltpu.PrefetchScalarGridSpec(
            num_scalar_prefetch=1,
            grid=(batch,),
            in_specs=[pl.BlockSpec((None, 2 * C_WIDTH, page), (lambda b, pt, j=j: (base + pt[b, j], 0, 0)))
                      for j in range(n_pages)]
            + [pl.BlockSpec((None, seq, C_WIDTH), lambda b, pt: (b, 0, 0)),
               pl.BlockSpec((None, 2 * C_WIDTH, LANES), lambda b, pt: (b, 0, 0))],
            out_specs=pl.BlockSpec((None, seq, C_WIDTH), lambda b, pt: (b, 0, 0)),
            scratch_shapes=[pltpu.VMEM((2 * C_WIDTH, past), F32), pltpu.VMEM((8, C_WIDTH), F32)]),
        out_shape=jax.ShapeDtypeStruct((batch, seq, C_WIDTH), F32),
        compiler_params=_cp("parallel"),
        name="moba_sample",
    )(page_table, *([pages_t] * n_pages), cq.reshape(batch, seq, C_WIDTH), tail)
    return o.reshape(batch * seq, C_WIDTH)


FF_CHUNK = 256


def _ffn_kernel(*refs, seq, tm, d_ff, paged_prev, tiles_per_seq):
    if paged_prev:
        (x_ref, oa_ref, ob_ref, oc_ref, g1_ref, sc2_ref, sh2_ref, g2_ref, n2_ref, wout_ref, wup_ref,
         cw_ref, cb_ref, wdn_ref, p0_ref, p1_ref, y_ref, u_ref, ubuf, acc_scr) = refs
    else:
        (x_ref, oa_ref, ob_ref, oc_ref, g1_ref, sc2_ref, sh2_ref, g2_ref, n2_ref, wout_ref, wup_ref,
         cw_ref, cb_ref, wdn_ref, y_ref, cst_ref, ubuf, acc_scr, carry) = refs
    i = pl.program_id(0)
    mix = (_dot(oa_ref[...], wout_ref[0:A_WIDTH, :])
           + _dot(ob_ref[...], wout_ref[A_WIDTH:A_WIDTH + B_W, :])
           + _dot(oc_ref[...], wout_ref[A_WIDTH + B_W:, :]))
    x1 = x_ref[...] + g1_ref[...] * mix
    h = x1 * lax.rsqrt(jnp.mean(x1 * x1, axis=-1, keepdims=True) + EPS) * n2_ref[...]
    h = (h * (1.0 + sc2_ref[...]) + sh2_ref[...]).astype(BF16)
    acc_scr[...] = jnp.zeros_like(acc_scr)
    if paged_prev:
        trow = _iota((tm, 1), 0) % seq
    else:
        @pl.when(i % tiles_per_seq == 0)
        def _():
            carry[...] = jnp.zeros_like(carry)
    tf = FF_CHUNK
    hdr = 8

    def conv_half(c0):
        c0 = pl.multiple_of(c0, tf)
        u = jnp.dot(h, wup_ref[:, pl.ds(c0, tf)], preferred_element_type=F32)
        ubuf[hdr:hdr + tm, :] = u
        if paged_prev:
            u_ref[:, pl.ds(c0, tf)] = u
            p0 = p0_ref[:, pl.ds(c0, tf)]
            p1 = p1_ref[:, pl.ds(c0, tf)]
            u1 = jnp.where(trow == 0, p1, ubuf[hdr - 1:hdr - 1 + tm, :])
            u2 = jnp.where(trow == 0, p0, jnp.where(trow == 1, p1, ubuf[hdr - 2:hdr - 2 + tm, :]))
        else:
            ubuf[hdr - 2:hdr, :] = carry[:, pl.ds(c0, tf)]
            u1 = ubuf[hdr - 1:hdr - 1 + tm, :]
            u2 = ubuf[hdr - 2:hdr - 2 + tm, :]
            carry[:, pl.ds(c0, tf)] = u[tm - 2:tm, :]
            cst_ref[:, pl.ds(c0, tf)] = u[tm - 2:tm, :]
        w = cw_ref[:, pl.ds(c0, tf)]
        return cb_ref[:, pl.ds(c0, tf)] + w[0:1, :] * u2 + w[1:2, :] * u1 + w[2:3, :] * u

    def chunk(c, carry_):
        f0 = c * tf
        a = conv_half(f0)
        gte = conv_half(d_ff + f0)
        act = (_silu(a) * gte).astype(BF16)
        acc_scr[...] += jnp.dot(act, wdn_ref[pl.ds(pl.multiple_of(f0, tf), tf), :], preferred_element_type=F32)
        return carry_

    lax.fori_loop(0, d_ff // tf, chunk, 0)
    y_ref[...] = x1 + g2_ref[...] * acc_scr[...]


def _ffn(x2, oa, ob, oc, mods, n2, wout_bf, wup_bf, conv_w, conv_b, wdn_bf, prev, *, tm, seq, rows_per_mod):
    n, d = x2.shape
    d_ff = wdn_bf.shape[0]
    assert d_ff % FF_CHUNK == 0 and n % tm == 0
    if rows_per_mod > 1:
        per = rows_per_mod // tm
        mod_spec = pl.BlockSpec((None, 1, d), lambda i: (i // per, 0, 0))
    else:
        mod_spec = pl.BlockSpec((tm, d), lambda i: (i, 0))
    row = lambda wd: pl.BlockSpec((tm, wd), lambda i: (i, 0))
    full = lambda a: pl.BlockSpec(a.shape, lambda i: (0,) * a.ndim)
    in_specs = [row(d), row(A_WIDTH), row(B_W), row(C_WIDTH), mod_spec, mod_spec, mod_spec, mod_spec,
                full(n2), full(wout_bf), full(wup_bf), full(conv_w), full(conv_b), full(wdn_bf)]
    args = [x2, oa, ob, oc, *mods, n2, wout_bf, wup_bf, conv_w, conv_b, wdn_bf]
    scratch = [pltpu.VMEM((tm + 8, FF_CHUNK), F32), pltpu.VMEM((tm, d), F32)]
    if prev is None:
        assert seq % tm == 0 and tm >= 2
        tiles_per_seq = seq // tm
        out_specs = [row(d), pl.BlockSpec((None, CONV_W - 1, 2 * d_ff), lambda i: (i // tiles_per_seq, 0, 0))]
        out_shape = [jax.ShapeDtypeStruct((n, d), F32), jax.ShapeDtypeStruct((n // seq, CONV_W - 1, 2 * d_ff), F32)]
        scratch.append(pltpu.VMEM((CONV_W - 1, 2 * d_ff), F32))
        sem = "arbitrary"
    else:
        assert tm % seq == 0 and seq >= 2
        tiles_per_seq = 1
        in_specs += [row(2 * d_ff), row(2 * d_ff)]
        args += list(prev)
        out_specs = [row(d), row(2 * d_ff)]
        out_shape = [jax.ShapeDtypeStruct((n, d), F32), jax.ShapeDtypeStruct((n, 2 * d_ff), F32)]
        sem = "parallel"
    kern = functools.partial(_ffn_kernel, seq=seq, tm=tm, d_ff=d_ff, paged_prev=prev is not None,
                             tiles_per_seq=tiles_per_seq)
    return pl.pallas_call(
        kern, grid=(n // tm,), in_specs=in_specs, out_specs=out_specs, out_shape=out_shape,
        scratch_shapes=scratch, compiler_params=_cp(sem), name="outproj_convffn",
    )(*args)


def _final_norm_kernel(x_ref, g_ref, o_ref):
    x = x_ref[...]
    o_ref[...] = x * lax.rsqrt(jnp.mean(x * x, axis=-1, keepdims=True) + EPS) * g_ref[...]


def _final_norm(x2, g, tm):
    n, d = x2.shape
    return pl.pallas_call(
        _final_norm_kernel, grid=(n // tm,),
        in_specs=[pl.BlockSpec((tm, d), lambda i: (i, 0)), pl.BlockSpec((1, d), lambda i: (0, 0))],
        out_specs=pl.BlockSpec((tm, d), lambda i: (i, 0)),
        out_shape=jax.ShapeDtypeStruct((n, d), F32),
        compiler_params=_cp("parallel"), name="final_norm",
    )(x2, g)


def _rope_tables(pos):
    half = HEAD_DIM // 2
    inv = ROPE_THETA ** (-jnp.arange(half, dtype=F32) / half)
    ang = pos.astype(F32)[:, None] * inv[None, :]
    cos = jnp.cos(ang)
    sin = jnp.sin(ang)
    return jnp.tile(cos, (1, 4)), jnp.concatenate([-sin, sin, -sin, sin], axis=1), cos.T, sin.T


def _split_w_in(w_in):
    c = 0
    aq = w_in[..., c:c + A_WIDTH]; c += A_WIDTH
    kv = w_in[..., c:c + 3 * KV_W]; c += 3 * KV_W
    gate = w_in[..., c:c + 3 * A_HEADS]; c += 3 * A_HEADS
    hg = w_in[..., c:c + 4 * B_W]; c += 4 * B_W
    cq = w_in[..., c:c + C_WIDTH]; c += C_WIDTH
    ckv = w_in[..., c:c + 2 * C_WIDTH]
    pad = jnp.zeros(w_in.shape[:2] + (GATE_PAD - 3 * A_HEADS,), w_in.dtype)
    w_row = jnp.concatenate([aq, gate, pad, hg, cq], axis=-1).astype(BF16)
    w_kvt = jnp.swapaxes(jnp.concatenate([kv, ckv], axis=-1), 1, 2).astype(BF16)
    return w_row, w_kvt


def _compress_weights(pe, pool, w1, w2, length):
    rep_t = lambda a: jnp.concatenate([a[0], a[0], a[1], a[1]], axis=-1).T
    pet, poolt = rep_t(pe), rep_t(pool)
    pa = jnp.tile(poolt[:, :CMP_STRIDE], (1, length // CMP_STRIDE))
    pb = jnp.tile(poolt[:, CMP_STRIDE:], (1, length // CMP_STRIDE))
    z1 = jnp.zeros_like(w1[0])
    z2 = jnp.zeros_like(w2[0])
    w1bd = jnp.concatenate([jnp.concatenate([w1[0], z1, z1, z1], 1), jnp.concatenate([z1, w1[0], z1, z1], 1),
                            jnp.concatenate([z1, z1, w1[1], z1], 1), jnp.concatenate([z1, z1, z1, w1[1]], 1)], 0)
    w2bd = jnp.concatenate([jnp.concatenate([w2[0], z2, z2, z2], 1), jnp.concatenate([z2, w2[0], z2, z2], 1),
                            jnp.concatenate([z2, z2, w2[1], z2], 1), jnp.concatenate([z2, z2, z2, w2[1]], 1)], 0)
    return pa, pb, pet, poolt, w1bd.T.astype(BF16), w2bd.T.astype(BF16)


def _rows_from_t(xt, heads):
    lead = xt.shape[:-2]
    t = xt.shape[-1]
    x = xt.reshape(lead + (2, heads, HEAD_DIM, t))
    return jnp.moveaxis(x, -1, len(lead))


def kernel(x_prompt, x_sample, cache_nsa_cmp_kv, cache_nsa_sel_kv, cache_nsa_win_kv, state_hgrn, cache_moba_kv, state_ffn_conv, page_table, c_prompt, c_sample, w_ada, b_ada, norm1_g, norm2_g, w_in, w_out, cmp_pe, cmp_pool, cmp_w1, cmp_w2, hgrn_lb_logits, hgrn_norm_g, w_up, conv_w, conv_b, w_down, final_g):
    bp, tp, d = x_prompt.shape
    bs, ts, _ = x_sample.shape
    depth = w_in.shape[0]
    d_ff = w_down.shape[1]
    n_pool, page = cache_moba_kv.shape[1], cache_moba_kv.shape[2]
    past = page_table.shape[1] * page
    np_, ns_ = bp * tp, bs * ts

    w_ada_bf = w_ada.astype(BF16)
    w_row, w_kvt = _split_w_in(w_in)
    w_out_bf = w_out.astype(BF16)
    w_up_bf = w_up.astype(BF16)
    w_dn_bf = w_down.astype(BF16)
    ng4 = jnp.tile(hgrn_norm_g, (1, B_HEADS))

    mod = _modulation(jnp.concatenate([c_prompt, c_sample], axis=0), w_ada_bf, b_ada)
    mod = mod.reshape(depth, bp + bs, 6, d)

    cos_p, sin_p, cost_p, sint_p = _rope_tables(jnp.arange(tp))
    cos_s, sin_s, cost_s, sint_s = _rope_tables(past + jnp.arange(ts))
    cos_s, sin_s = jnp.tile(cos_s, (bs, 1)), jnp.tile(sin_s, (bs, 1))
    cost_s, sint_s = jnp.tile(cost_s, (1, bs)), jnp.tile(sint_s, (1, bs))

    def pages_t(c):
        w = c.shape[3] * c.shape[4] * c.shape[5]
        return jnp.transpose(c, (0, 1, 3, 4, 5, 2)).reshape(c.shape[0] * c.shape[1], w, c.shape[2])

    cmp_pages = pages_t(cache_nsa_cmp_kv)
    sel_pages = pages_t(cache_nsa_sel_kv)
    moba_pages = pages_t(cache_moba_kv)
    win_cache = pages_t(cache_nsa_win_kv)

    def tails(xt):
        c = xt.shape[1]
        x = jnp.transpose(xt.reshape(c, bs, ts), (1, 0, 2))
        return jnp.pad(x, ((0, 0), (0, 0), (LANES - ts, 0)))

    tm_p = min(512, tp)
    tm_f = min(512, tp)
    tm_fs = min(128, ns_)
    xp = x_prompt.reshape(np_, d)
    xs = x_sample.reshape(ns_, d)
    st_p, st_s = [], []
    zero_state = jnp.zeros((bp, B_W, B_W), F32)
    w_keep = min(WINDOW, tp)
    for l in range(depth):
        mp = [mod[l, :bp, j].reshape(bp, 1, d) for j in range(6)]
        ms = [jnp.repeat(mod[l, bp:, j], ts, axis=0) for j in range(6)]
        n1 = norm1_g[l].reshape(1, d)
        n2 = norm2_g[l].reshape(1, d)
        cw_p = _compress_weights(cmp_pe[l], cmp_pool[l], cmp_w1[l], cmp_w2[l], tp)
        cw_s = cw_p if past == tp else _compress_weights(cmp_pe[l], cmp_pool[l], cmp_w1[l], cmp_w2[l], past)

        qa, qar, gate, hg, cq, cmpt, selt, wint, mobat = _in_projection(
            xp, mp[1], mp[0], n1, w_row[l], w_kvt[l], cos_p, sin_p, cost_p, sint_p,
            tm=tm_p, seq=tp, rows_per_mod=tp, rows_per_pos=tp)
        o_a = _nsa_prompt(qa, qar, gate, cmpt, selt, wint, cw_p, batch=bp, seq=tp)
        o_b, st_new = _hgrn(hg, zero_state, hgrn_lb_logits, ng4[l:l + 1], layer=l, batch=bp, seq=tp)
        o_c = _moba_prompt(cq, mobat, batch=bp, seq=tp)
        xp, conv_st = _ffn(xp, o_a, o_b, o_c, (mp[2], mp[4], mp[3], mp[5]), n2, w_out_bf[l], w_up_bf[l],
                           conv_w[l], conv_b[l].reshape(1, -1), w_dn_bf[l], None, tm=tm_f, seq=tp, rows_per_mod=tp)
        st_p.append((cmpt, selt, wint[:, :, tp - w_keep:], _blockdiag_to_state(st_new), mobat, conv_st))

        qa, qar, gate, hg, cq, cmpt, selt, wint, mobat = _in_projection(
            xs, ms[1], ms[0], n1, w_row[l], w_kvt[l], cos_s, sin_s, cost_s, sint_s,
            tm=ns_, seq=ns_, rows_per_mod=1, rows_per_pos=ns_)
        o_a, win_state = _nsa_sample(qa, qar, gate, tails(selt), tails(wint), cmp_pages, sel_pages, win_cache,
                                     page_table, cw_s, layer=l, depth=depth, batch=bs, seq=ts)
        o_b, st_new = _hgrn(hg, _state_to_blockdiag(state_hgrn[l]), hgrn_lb_logits, ng4[l:l + 1],
                            layer=l, batch=bs, seq=ts)
        o_c = _moba_sample(cq, tails(mobat), moba_pages, page_table, layer=l, depth=depth, batch=bs, seq=ts)
        prev = state_ffn_conv[l]
        p0 = jnp.repeat(prev[:, 0], ts, axis=0)
        p1 = jnp.repeat(prev[:, 1], ts, axis=0)
        xs, u_s = _ffn(xs, o_a, o_b, o_c, (ms[2], ms[4], ms[3], ms[5]), n2, w_out_bf[l], w_up_bf[l],
                       conv_w[l], conv_b[l].reshape(1, -1), w_dn_bf[l], (p0, p1), tm=tm_fs, seq=ts, rows_per_mod=1)
        full = jnp.concatenate([prev, u_s.reshape(bs, ts, 2 * d_ff)], axis=1)
        new_t = lambda xt: jnp.transpose(xt.reshape(xt.shape[1], bs, ts), (1, 0, 2))
        st_s.append((new_t(cmpt), new_t(selt), win_state, _blockdiag_to_state(st_new), new_t(mobat),
                     full[:, -(CONV_W - 1):]))

    fg = final_g.reshape(1, d)
    y_p = _final_norm(xp, fg, tm_p).reshape(bp, tp, d)
    y_s = _final_norm(xs, fg, ns_).reshape(bs, ts, d)

    def stacked(sts, j):
        return jnp.stack([st[j] for st in sts], axis=0)

    return (y_p, y_s,
            _rows_from_t(stacked(st_p, 0), A_KV_HEADS), _rows_from_t(stacked(st_p, 1), A_KV_HEADS),
            _rows_from_t(stacked(st_p, 2), A_KV_HEADS), stacked(st_p, 3),
            _rows_from_t(stacked(st_p, 4), C_HEADS), stacked(st_p, 5),
            _rows_from_t(stacked(st_s, 0), A_KV_HEADS), _rows_from_t(stacked(st_s, 1), A_KV_HEADS),
            _rows_from_t(stacked(st_s, 2), A_KV_HEADS), stacked(st_s, 3),
            _rows_from_t(stacked(st_s, 4), C_HEADS), stacked(st_s, 5))
```

```python
import functools

import jax
import jax.numpy as jnp
from jax import lax
from jax.experimental import pallas as pl
from jax.experimental.pallas import tpu as pltpu

F32 = jnp.float32
BF16 = jnp.bfloat16

HEAD_DIM = 64
ROPE_THETA = 10000.0
EPS = 1e-6
NEG = -1e30
A_HEADS = 6
A_KV_HEADS = 2
A_REP = A_HEADS // A_KV_HEADS
CMP_LEN = 32
CMP_STRIDE = 16
CMP_HIDDEN = 128
SEL_BLOCK = 64
SEL_TOP = 16
FORCE_BONUS = 1e4
WINDOW = 512
B_HEADS = 4
B_DK = 64
B_DV = 64
C_HEADS = 6
MOBA_BLOCK = 256
MOBA_TOP = 3
CONV_W = 3

LANES = 128
HG_CHUNK = 16
HG_UNROLL = 4
Q_TILE = 256
MOBA_Q_TILE = 512
VMEM_LIMIT = 56 * 1024 * 1024

A_WIDTH = A_HEADS * HEAD_DIM
KV_W = 2 * A_KV_HEADS * HEAD_DIM
B_W = B_HEADS * B_DK
C_WIDTH = C_HEADS * HEAD_DIM
GATE_PAD = LANES
RO_QA = 0
RO_GATE = RO_QA + A_WIDTH
RO_HG = RO_GATE + GATE_PAD
RO_CQ = RO_HG + 4 * B_W
ROW_W = RO_CQ + C_WIDTH
TO_CMP = 0
TO_SEL = TO_CMP + KV_W
TO_WIN = TO_SEL + KV_W
TO_MK = TO_WIN + KV_W
TO_MV = TO_MK + C_WIDTH
KVT_W = TO_MV + C_WIDTH
ATT_SCALE = HEAD_DIM ** -0.5
assert ATT_SCALE == 0.125
KEY_CHUNK = 512


def _cp(*sem):
    return pltpu.CompilerParams(dimension_semantics=sem, vmem_limit_bytes=VMEM_LIMIT)


def _dot(a, b):
    return jnp.dot(a.astype(BF16), b.astype(BF16), preferred_element_type=F32)


def _dot_nt(a, b):
    return lax.dot_general(a.astype(BF16), b.astype(BF16), (((1,), (1,)), ((), ())),
                           preferred_element_type=F32)


def _dot_tn(a, b):
    return lax.dot_general(a.astype(BF16), b.astype(BF16), (((0,), (0,)), ((), ())),
                           preferred_element_type=F32)


def _split2(x):
    hi = x.astype(BF16)
    lo = (x - hi.astype(F32)).astype(BF16)
    return hi, lo


def _dot_x2(x, w_bf16):
    hi, lo = _split2(x)
    return (jnp.dot(hi, w_bf16, preferred_element_type=F32)
            + jnp.dot(lo, w_bf16, preferred_element_type=F32))


def _sigmoid(x):
    return 1.0 / (1.0 + jnp.exp(-x))


def _silu(x):
    return x * _sigmoid(x)


def _masked_softmax(s, mask):
    s = jnp.where(mask, s, NEG)
    m = jnp.max(s, axis=-1, keepdims=True)
    p = jnp.where(mask, jnp.exp(s - m), 0.0)
    return p / jnp.maximum(jnp.sum(p, axis=-1, keepdims=True), 1e-30)


def _iota(shape, dim):
    return lax.broadcasted_iota(jnp.int32, shape, dim)


def _rope128(x, cos, sin):
    lane = _iota(x.shape, 1)
    first = (lane & (HEAD_DIM - 1)) < (HEAD_DIM // 2)
    swapped = jnp.where(first, pltpu.roll(x, LANES - HEAD_DIM // 2, 1), pltpu.roll(x, HEAD_DIM // 2, 1))
    return x * cos + swapped * sin


def _rank_before(score, n):
    lane = _iota((1, score.shape[1]), 1)
    rank = jnp.zeros(score.shape, F32)
    for j in range(n):
        col = score[:, j:j + 1]
        after_j = jnp.where(lane > j, 1.0, 0.0)
        rank = rank + jnp.where(col > score, 1.0, jnp.where(col == score, after_j, 0.0))
    return rank


def _mod_kernel(c_ref, w_ref, b_ref, o_ref):
    o_ref[...] = _dot(_silu(c_ref[...]), w_ref[...]) + b_ref[...]


def _modulation(c_all, w_ada_bf, b_ada):
    depth, d, n6 = w_ada_bf.shape
    nb = c_all.shape[0]
    tn = n6 // 4
    return pl.pallas_call(
        _mod_kernel,
        grid=(depth, n6 // tn),
        in_specs=[pl.BlockSpec((nb, d), lambda l, j: (0, 0)),
                  pl.BlockSpec((None, d, tn), lambda l, j: (l, 0, j)),
                  pl.BlockSpec((None, 1, tn), lambda l, j: (l, 0, j))],
        out_specs=pl.BlockSpec((None, nb, tn), lambda l, j: (l, 0, j)),
        out_shape=jax.ShapeDtypeStruct((depth, nb, n6), F32),
        compiler_params=_cp("parallel", "parallel"),
        name="adaln_mod",
    )(c_all, w_ada_bf, b_ada.reshape(depth, 1, n6))


def _inproj_kernel(x_ref, sc_ref, sh_ref, g_ref, wrow_ref, wkvt_ref, cos_ref, sin_ref, cost_ref, sint_ref,
                   qa_ref, qar_ref, gate_ref, hg_ref, cq_ref, cmpt_ref, selt_ref, wint_ref, mobat_ref):
    x = x_ref[...]
    h = x * lax.rsqrt(jnp.mean(x * x, axis=-1, keepdims=True) + EPS) * g_ref[...]
    h = (h * (1.0 + sc_ref[...]) + sh_ref[...]).astype(BF16)
    cos = cos_ref[...]
    sin = sin_ref[...]
    cost = cost_ref[...]
    sint = sint_ref[...]

    def proj(c0, n):
        return jnp.dot(h, wrow_ref[:, c0:c0 + n], preferred_element_type=F32)

    def proj_t(r0, n):
        return lax.dot_general(wkvt_ref[r0:r0 + n, :], h, (((1,), (1,)), ((), ())), preferred_element_type=F32)

    def rope_cols(y):
        return jnp.concatenate([_rope128(y[:, j:j + LANES], cos, sin) for j in range(0, y.shape[1], LANES)], axis=1)

    def rope_rows(y):
        hh = HEAD_DIM // 2
        parts = []
        for r0 in range(0, y.shape[0], HEAD_DIM):
            x1 = y[r0:r0 + hh, :]
            x2 = y[r0 + hh:r0 + HEAD_DIM, :]
            parts += [x1 * cost - x2 * sint, x2 * cost + x1 * sint]
        return jnp.concatenate(parts, axis=0)

    qa = proj(RO_QA, A_WIDTH)
    qa_ref[...] = qa
    qar_ref[...] = rope_cols(qa)
    gate_ref[...] = proj(RO_GATE, GATE_PAD)
    hg_ref[...] = proj(RO_HG, 4 * B_W)
    cq_ref[...] = rope_cols(proj(RO_CQ, C_WIDTH))
    half = KV_W // 2
    cmpt_ref[...] = proj_t(TO_CMP, KV_W)
    selt_ref[0:half, :] = rope_rows(proj_t(TO_SEL, half))
    selt_ref[half:, :] = proj_t(TO_SEL + half, half)
    wint_ref[0:half, :] = rope_rows(proj_t(TO_WIN, half))
    wint_ref[half:, :] = proj_t(TO_WIN + half, half)
    mobat_ref[0:C_WIDTH, :] = rope_rows(proj_t(TO_MK, C_WIDTH))
    mobat_ref[C_WIDTH:, :] = proj_t(TO_MV, C_WIDTH)


def _in_projection(x2, sc, sh, gain, w_row, w_kvt, cos, sin, cost, sint, *, tm, seq, rows_per_mod, rows_per_pos):
    n, d = x2.shape
    assert n % seq == 0 and seq % tm == 0 and rows_per_pos % tm == 0
    nseq = n // seq
    tps = seq // tm
    pos_blocks = rows_per_pos // tm
    if rows_per_mod > 1:
        per = rows_per_mod // tm
        mod_spec = pl.BlockSpec((None, 1, d), lambda i: (i // per, 0, 0))
    else:
        mod_spec = pl.BlockSpec((tm, d), lambda i: (i, 0))
    row_w = (A_WIDTH, A_WIDTH, GATE_PAD, 4 * B_W, C_WIDTH)
    t_w = (KV_W, KV_W, KV_W, 2 * C_WIDTH)
    full = lambda a: pl.BlockSpec(a.shape, lambda i: (0,) * a.ndim)
    return pl.pallas_call(
        _inproj_kernel,
        grid=(n // tm,),
        in_specs=[pl.BlockSpec((tm, d), lambda i: (i, 0)), mod_spec, mod_spec, full(gain), full(w_row), full(w_kvt),
                  pl.BlockSpec((tm, LANES), lambda i: (i % pos_blocks, 0)),
                  pl.BlockSpec((tm, LANES), lambda i: (i % pos_blocks, 0)),
                  pl.BlockSpec((HEAD_DIM // 2, tm), lambda i: (0, i % pos_blocks)),
                  pl.BlockSpec((HEAD_DIM // 2, tm), lambda i: (0, i % pos_blocks))],
        out_specs=[pl.BlockSpec((tm, wd), lambda i: (i, 0)) for wd in row_w]
        + [pl.BlockSpec((None, wd, tm), lambda i: (i // tps, 0, i % tps)) for wd in t_w],
        out_shape=[jax.ShapeDtypeStruct((n, wd), F32) for wd in row_w]
        + [jax.ShapeDtypeStruct((nseq, wd, seq), F32) for wd in t_w],
        compiler_params=_cp("parallel"),
        name="in_projection",
    )(x2, sc, sh, gain, w_row, w_kvt, cos, sin, cost, sint)


def _hgrn_kernel(hg_ref, st0_ref, lbl_ref, ng_ref, o_ref, st_out_ref,
                 st_scr, b_scr, key_scr, q_scr, v_scr, g_scr, o_scr, *, layer, valid, rows, nt):
    t = pl.program_id(1)

    @pl.when(t == 0)
    def _():
        st_scr[...] = st0_ref[...]

    lg = lbl_ref[...]
    e = jnp.exp(lg - jnp.max(lg, axis=0, keepdims=True))
    p = e / jnp.sum(e, axis=0, keepdims=True)
    acc = p[0:1, :]
    for i in range(1, layer + 1):
        acc = acc + p[i:i + 1, :]
    lb = jnp.clip(acc - p[0:1, :], 0.0, 1.0)
    lbpos = lb > 0.0

    if valid < rows:
        g_scr[...] = jnp.zeros_like(g_scr)
        v_scr[...] = jnp.zeros_like(v_scr)
        q_scr[...] = jnp.zeros_like(q_scr)
        key_scr[...] = jnp.zeros_like(key_scr)
        q_scr[0:valid, :] = hg_ref[:, 0:B_W]
        key_scr[0:valid, :] = hg_ref[:, B_W:2 * B_W]
        v_scr[0:valid, :] = hg_ref[:, 2 * B_W:3 * B_W]
        g_scr[0:valid, :] = hg_ref[:, 3 * B_W:4 * B_W]
        q = q_scr[...]
        z = key_scr[...]
    else:
        q = hg_ref[:, 0:B_W]
        z = hg_ref[:, B_W:2 * B_W]
        v_scr[...] = hg_ref[:, 2 * B_W:3 * B_W]
        g_scr[...] = hg_ref[:, 3 * B_W:4 * B_W]

    ls = jnp.minimum(z, 0.0) - jnp.log1p(jnp.exp(-jnp.abs(z)))
    c_ = jnp.log1p(-lb) + ls
    a_ = jnp.log(jnp.where(lbpos, lb, 1.0))
    logf = jnp.where(lbpos, jnp.maximum(a_, c_) + jnp.log1p(jnp.exp(-jnp.abs(a_ - c_))), c_)
    key = (1.0 - lb) * _sigmoid(-z)
    rowi = _iota((rows, 1), 0)
    live = rowi < valid
    logf = jnp.where(live, logf, 0.0)
    key = jnp.where(live, key, 0.0)
    ri = _iota((rows, rows), 0)
    ci = _iota((rows, rows), 1)
    tri = jnp.where(((ri // HG_CHUNK) == (ci // HG_CHUNK)) & (ci <= ri), 1.0, 0.0).astype(BF16)
    h1 = logf.astype(BF16)
    r1 = logf - h1.astype(F32)
    h2 = r1.astype(BF16)
    h3 = (r1 - h2.astype(F32)).astype(BF16)
    b_scr[...] = (jnp.dot(tri, h1, preferred_element_type=F32) + jnp.dot(tri, h2, preferred_element_type=F32)
                  + jnp.dot(tri, h3, preferred_element_type=F32))
    key_scr[...] = key
    q_scr[...] = _silu(q)

    hr = _iota((B_W, B_W), 0)
    hc = _iota((B_W, B_W), 1)
    same_head = (hr // B_DK) == (hc // B_DK)
    head_ones = jnp.where(same_head, 1.0, 0.0).astype(BF16)
    ti = _iota((HG_CHUNK, 1), 0)
    ng = ng_ref[...]

    def chunk(c, carry):
        r0 = pl.multiple_of(c * HG_CHUNK, HG_CHUNK)
        b = b_scr[pl.ds(r0, HG_CHUNK), :]
        kk = key_scr[pl.ds(r0, HG_CHUNK), :]
        qf = q_scr[pl.ds(r0, HG_CHUNK), :]
        v = v_scr[pl.ds(r0, HG_CHUNK), :]
        gt = g_scr[pl.ds(r0, HG_CHUNK), :]
        st = st_scr[...]
        blast = b[HG_CHUNK - 1:HG_CHUNK, :]
        o_inter = _dot_nt(qf * jnp.exp(b), st)
        es = []
        for s in range(HG_CHUNK):
            e_s = qf * kk[s:s + 1, :] * jnp.exp(jnp.minimum(b - b[s:s + 1, :], 0.0))
            es.append(jnp.where(ti >= s, e_s, 0.0))
        r = _dot_x2(jnp.concatenate(es, axis=0), head_ones)
        o = o_inter
        for s in range(HG_CHUNK):
            o = o + r[s * HG_CHUNK:(s + 1) * HG_CHUNK, :] * v[s:s + 1, :]
        ms = _dot_x2(o * o, head_ones) * (1.0 / B_DV)
        o_scr[pl.ds(r0, HG_CHUNK), :] = o * lax.rsqrt(ms + EPS) * ng * _silu(gt)
        upd = _dot_tn(v, kk * jnp.exp(blast - b))
        st_scr[...] = st * jnp.exp(blast) + jnp.where(same_head, upd, 0.0)
        return carry

    n_chunks = rows // HG_CHUNK
    lax.fori_loop(0, n_chunks, chunk, 0, unroll=min(HG_UNROLL, n_chunks))
    o_ref[...] = o_scr[0:valid, :]

    @pl.when(t == nt - 1)
    def _():
        st_out_ref[...] = st_scr[...]


def _hgrn(hg, st0, lb_logits, norm_g4, *, layer, batch, seq):
    if seq % HG_CHUNK == 0:
        tt = min(seq, 256)
        assert seq % tt == 0
        nt = seq // tt
        rows = valid = tt
        hg_in = hg
        hg_spec = pl.BlockSpec((tt, 4 * B_W), lambda b, t: (b * nt + t, 0))
        o_spec = pl.BlockSpec((tt, B_W), lambda b, t: (b * nt + t, 0))
        o_shape = jax.ShapeDtypeStruct((batch * seq, B_W), F32)
    else:
        assert seq < HG_CHUNK
        nt, rows, valid = 1, HG_CHUNK, seq
        hg_in = hg.reshape(batch, seq, 4 * B_W)
        hg_spec = pl.BlockSpec((None, seq, 4 * B_W), lambda b, t: (b, 0, 0))
        o_spec = pl.BlockSpec((None, seq, B_W), lambda b, t: (b, 0, 0))
        o_shape = jax.ShapeDtypeStruct((batch, seq, B_W), F32)
    depth = lb_logits.shape[0]
    kern = functools.partial(_hgrn_kernel, layer=layer, valid=valid, rows=rows, nt=nt)
    o, st = pl.pallas_call(
        kern,
        grid=(batch, nt),
        in_specs=[hg_spec,
                  pl.BlockSpec((None, B_W, B_W), lambda b, t: (b, 0, 0)),
                  pl.BlockSpec((depth, B_W), lambda b, t: (0, 0)),
                  pl.BlockSpec((1, B_W), lambda b, t: (0, 0))],
        out_specs=[o_spec, pl.BlockSpec((None, B_W, B_W), lambda b, t: (b, 0, 0))],
        out_shape=[o_shape, jax.ShapeDtypeStruct((batch, B_W, B_W), F32)],
        scratch_shapes=[pltpu.VMEM((B_W, B_W), F32)] + [pltpu.VMEM((rows, B_W), F32)] * 6,
        compiler_params=_cp("parallel", "arbitrary"),
        name="hgrn2",
    )(hg_in, st0, lb_logits, norm_g4)
    return o.reshape(batch * seq, B_W), st


def _state_to_blockdiag(s0):
    s0t = jnp.swapaxes(s0, 2, 3)
    z = jnp.zeros_like(s0t[:, 0])
    rows = [jnp.concatenate([s0t[:, h] if g == h else z for g in range(B_HEADS)], axis=-1) for h in range(B_HEADS)]
    return jnp.concatenate(rows, axis=1)


def _blockdiag_to_state(st):
    b = st.shape[0]
    st5 = st.reshape(b, B_HEADS, B_DV, B_HEADS, B_DK)
    diag = jnp.stack([st5[:, h, :, h, :] for h in range(B_HEADS)], axis=1)
    return jnp.swapaxes(diag, 2, 3)


def _tile_rows(x, n):
    return jnp.concatenate([x] * n, axis=0)


def _softmax_pv(q, kt, vt, bias):
    s = _dot(q * ATT_SCALE, kt) + bias
    p = jnp.exp(s - jnp.max(s, axis=-1, keepdims=True))
    return _dot_nt(p, vt) / jnp.sum(p, axis=-1, keepdims=True)


def _softmax_pv2(q, kt1, vt1, bias1, kt2, vt2, bias2):
    qs = q * ATT_SCALE
    s1 = _dot(qs, kt1) + bias1
    s2 = _dot(qs, kt2) + bias2
    m = jnp.maximum(jnp.max(s1, axis=-1, keepdims=True), jnp.max(s2, axis=-1, keepdims=True))
    p1 = jnp.exp(s1 - m)
    p2 = jnp.exp(s2 - m)
    den = jnp.sum(p1, axis=-1, keepdims=True) + jnp.sum(p2, axis=-1, keepdims=True)
    return (_dot_nt(p1, vt1) + _dot_nt(p2, vt2)) / den


def _compress_t(xt, pa, pb, pet, poolt, w1t, w2t, nblk):
    length = xt.shape[1]
    tb = _iota((length, nblk), 0) // CMP_STRIDE
    nb = _iota((length, nblk), 1)
    first = jnp.where(tb == nb, 1.0, 0.0).astype(BF16)
    second = jnp.where(tb == nb + 1, 1.0, 0.0).astype(BF16)
    bias = jnp.sum(pet * poolt, axis=1, keepdims=True)
    pooled = _dot(xt * pa, first) + _dot(xt * pb, second) + bias
    hid = _silu(jnp.dot(w1t, pooled.astype(BF16), preferred_element_type=F32))
    return jnp.dot(w2t, hid.astype(BF16), preferred_element_type=F32)


def _nsa_select(qa, cmpt, qpos, *, n_cmp, n_sel):
    tq = qa.shape[0]
    ncp = cmpt.shape[1]
    n_i = _iota((1, ncp), 1)
    maskf = jnp.where((n_i * CMP_STRIDE + (CMP_LEN - 1) <= qpos) & (n_i < n_cmp), 1.0, 0.0)
    mask3 = _tile_rows(maskf, A_REP) > 0.5
    on = _iota((ncp, LANES), 0) * CMP_STRIDE
    os_ = _iota((ncp, LANES), 1) * SEL_BLOCK
    overlap = jnp.where((on < os_ + SEL_BLOCK) & (on + CMP_LEN > os_), 1.0, 0.0).astype(BF16)
    s_i = _iota((1, LANES), 1)
    cur = qpos // SEL_BLOCK
    forced = (s_i == 0) | (s_i == cur) | (s_i == cur - 1)
    visible = (s_i * SEL_BLOCK <= qpos) & (s_i < n_sel)
    half = KV_W // 2
    o_cmp, chosen = [], []
    for g in range(A_KV_HEADS):
        kct = cmpt[g * HEAD_DIM:(g + 1) * HEAD_DIM, :]
        vct = cmpt[half + g * HEAD_DIM:half + (g + 1) * HEAD_DIM, :]
        q3 = _stack_heads(qa, g)
        p = _masked_softmax(_dot(q3, kct) * ATT_SCALE, mask3)
        o_cmp.append(_dot_nt(p, vct))
        p_sum = p[0:tq]
        for r in range(1, A_REP):
            p_sum = p_sum + p[r * tq:(r + 1) * tq]
        imp = _dot_x2(p_sum, overlap)
        score = jnp.where(visible, imp + jnp.where(forced, FORCE_BONUS, 0.0), NEG)
        chosen.append(jnp.where(_rank_before(score, n_sel) < min(SEL_TOP, n_sel), 1.0, 0.0))
    return o_cmp, chosen


def _stack_heads(q, g):
    return jnp.concatenate([q[:, (g * A_REP + r) * HEAD_DIM:(g * A_REP + r + 1) * HEAD_DIM] for r in range(A_REP)], axis=0)


def _compress_kernel(xt_ref, pa_ref, pb_ref, pet_ref, poolt_ref, w1t_ref, w2t_ref, o_ref, *, nblk):
    o_ref[...] = _compress_t(xt_ref[...], pa_ref[...], pb_ref[...], pet_ref[...], poolt_ref[...],
                             w1t_ref[...], w2t_ref[...], nblk)


def _nsa_prompt_kernel(qa_ref, qar_ref, gate_ref, cmp_ref, selt_ref, wint_ref, o_ref, part_scr, ch_scr,
                       *, seq, tq, n_cmp, n_sel, span, kchunk):
    q0 = pl.program_id(1) * tq
    qpos = q0 + _iota((tq, 1), 0)
    gs = _sigmoid(gate_ref[...])
    qar = qar_ref[...]
    o_cmp, chosen = _nsa_select(qa_ref[...], cmp_ref[...], qpos, n_cmp=n_cmp, n_sel=n_sel)
    start = pl.multiple_of(jnp.clip(q0 - WINDOW, 0, seq - span), LANES)
    kpos_w = start + _iota((1, span), 1)
    wbias = _tile_rows(jnp.where(kpos_w <= qpos, jnp.where(qpos - kpos_w <= WINDOW, 0.0, NEG), NEG), A_REP)
    half = KV_W // 2
    parts = []
    for g in range(A_KV_HEADS):
        q3 = _stack_heads(qar, g)
        o_win = _softmax_pv(q3, wint_ref[g * HEAD_DIM:(g + 1) * HEAD_DIM, pl.ds(start, span)],
                            wint_ref[half + g * HEAD_DIM:half + (g + 1) * HEAD_DIM, pl.ds(start, span)], wbias)
        for r in range(A_REP):
            c0 = 3 * (g * A_REP + r)
            parts.append(gs[:, c0:c0 + 1] * o_cmp[g][r * tq:(r + 1) * tq] + gs[:, c0 + 2:c0 + 3] * o_win[r * tq:(r + 1) * tq])
        ch_scr[g] = chosen[g]
    part_scr[...] = jnp.concatenate(parts, axis=1)
    bucket = (q0 + tq - 1) // kchunk
    for k in range(seq // kchunk):
        @pl.when(bucket == k)
        def _(k=k):
            klen = (k + 1) * kchunk
            expand = jnp.where((_iota((LANES, klen), 1) // SEL_BLOCK) == _iota((LANES, klen), 0), 1.0, 0.0).astype(BF16)
            causal_bias = jnp.where(_iota((1, klen), 1) <= qpos, 0.0, NEG)
            outs = []
            for g in range(A_KV_HEADS):
                ck = jnp.dot(ch_scr[g].astype(BF16), expand, preferred_element_type=F32)
                bias = _tile_rows(jnp.where(ck > 0.5, causal_bias, NEG), A_REP)
                o_sel = _softmax_pv(_stack_heads(qar, g), selt_ref[g * HEAD_DIM:(g + 1) * HEAD_DIM, 0:klen],
                                    selt_ref[half + g * HEAD_DIM:half + (g + 1) * HEAD_DIM, 0:klen], bias)
                for r in range(A_REP):
                    c0 = 3 * (g * A_REP + r)
                    outs.append(gs[:, c0 + 1:c0 + 2] * o_sel[r * tq:(r + 1) * tq])
            o_ref[...] = part_scr[...] + jnp.concatenate(outs, axis=1)


def _nsa_prompt(qa, qar, gate, cmpt, selt, wint, cw, *, batch, seq):
    pa, pb, pet, poolt, w1t, w2t = cw
    assert seq % Q_TILE == 0 and seq % CMP_STRIDE == 0 and seq >= CMP_LEN
    nblk = seq // CMP_STRIDE
    n_cmp = (seq - CMP_LEN) // CMP_STRIDE + 1
    n_sel = -(-seq // SEL_BLOCK)
    assert nblk <= LANES and n_sel <= LANES and n_cmp == nblk - 1
    tq = Q_TILE
    nq = seq // tq
    span = min(seq, tq + WINDOW)
    kchunk = min(KEY_CHUNK, seq)
    assert seq % kchunk == 0 and kchunk % tq == 0
    full = lambda a: pl.BlockSpec(a.shape, lambda b: (0,) * a.ndim)
    cmpkv = pl.pallas_call(
        functools.partial(_compress_kernel, nblk=nblk),
        grid=(batch,),
        in_specs=[pl.BlockSpec((None, KV_W, seq), lambda b: (b, 0, 0)),
                  full(pa), full(pb), full(pet), full(poolt), full(w1t), full(w2t)],
        out_specs=pl.BlockSpec((None, KV_W, nblk), lambda b: (b, 0, 0)),
        out_shape=jax.ShapeDtypeStruct((batch, KV_W, nblk), F32),
        compiler_params=_cp("parallel"),
        name="nsa_compress",
    )(cmpt, pa, pb, pet, poolt, w1t, w2t)
    kern = functools.partial(_nsa_prompt_kernel, seq=seq, tq=tq, n_cmp=n_cmp, n_sel=n_sel, span=span, kchunk=kchunk)
    return pl.pallas_call(
        kern,
        grid=(batch, nq),
        in_specs=[pl.BlockSpec((tq, A_WIDTH), lambda b, i: (b * nq + i, 0)),
                  pl.BlockSpec((tq, A_WIDTH), lambda b, i: (b * nq + i, 0)),
                  pl.BlockSpec((tq, GATE_PAD), lambda b, i: (b * nq + i, 0)),
                  pl.BlockSpec((None, KV_W, nblk), lambda b, i: (b, 0, 0)),
                  pl.BlockSpec((None, KV_W, seq), lambda b, i: (b, 0, 0)),
                  pl.BlockSpec((None, KV_W, seq), lambda b, i: (b, 0, 0))],
        out_specs=pl.BlockSpec((tq, A_WIDTH), lambda b, i: (b * nq + i, 0)),
        out_shape=jax.ShapeDtypeStruct((batch * seq, A_WIDTH), F32),
        scratch_shapes=[pltpu.VMEM((tq, A_WIDTH), F32), pltpu.VMEM((A_KV_HEADS, tq, LANES), F32)],
        compiler_params=_cp("parallel", "parallel"),
        name="nsa_prompt",
    )(qa, qar, gate, cmpkv, selt, wint)


def _nsa_sample_kernel(pt_ref, *refs, n_pages, page, seq, past, wb):
    cmp_pages = refs[:n_pages]
    sel_pages = refs[n_pages:2 * n_pages]
    (wcache_ref, qa_ref, qar_ref, gate_ref, stail_ref, wtail_ref, pa_ref, pb_ref, pet_ref, poolt_ref, w1t_ref, w2t_ref,
     o_ref, wstate_ref, cmp_scr, sel_scr, q_scr) = refs[2 * n_pages:]
    del pt_ref
    for j in range(n_pages):
        cmp_scr[:, j * page:(j + 1) * page] = cmp_pages[j][...]
        sel_scr[:, j * page:(j + 1) * page] = sel_pages[j][...]
    tq = q_scr.shape[1]
    q_scr[...] = jnp.zeros_like(q_scr)
    q_scr[0, 0:seq, :] = qa_ref[...]
    q_scr[1, 0:seq, :] = qar_ref[...]
    q_scr[2, 0:seq, 0:GATE_PAD] = gate_ref[...]
    nblk = past // CMP_STRIDE
    total = past + seq
    n_cmp = (total - CMP_LEN) // CMP_STRIDE + 1
    n_sel = -(-total // SEL_BLOCK)
    new_blk = past // SEL_BLOCK
    cmpt = _compress_t(cmp_scr[...], pa_ref[...], pb_ref[...], pet_ref[...], poolt_ref[...],
                       w1t_ref[...], w2t_ref[...], nblk)
    half = KV_W // 2
    zero = jnp.zeros((tq, HEAD_DIM), F32)

    def stack_bd(q):
        blocks = []
        for h in range(A_HEADS):
            qh = q[:, h * HEAD_DIM:(h + 1) * HEAD_DIM]
            blocks.append(jnp.concatenate([qh, zero] if h < A_REP else [zero, qh], axis=1))
        return jnp.concatenate(blocks, axis=0)

    def per_group(x):
        return jnp.concatenate([x[0:tq]] * A_REP + [x[tq:2 * tq]] * A_REP, axis=0)

    rows = A_HEADS * tq
    trow = _iota((rows, 1), 0) % tq
    qpos = past + trow
    qa_bd = stack_bd(q_scr[0])
    qar_bd = stack_bd(q_scr[1])
    gs = _sigmoid(q_scr[2][:, 0:GATE_PAD])
    n_i = _iota((1, nblk), 1)
    cmp_mask = (n_i * CMP_STRIDE + (CMP_LEN - 1) <= qpos) & (n_i < n_cmp)
    p = _masked_softmax(_dot(qa_bd, cmpt[0:half, :]) * ATT_SCALE, cmp_mask)
    o_cmp = _dot_nt(p, cmpt[half:, :])
    p_sum = jnp.concatenate([p[g * A_REP * tq:g * A_REP * tq + tq] + p[g * A_REP * tq + tq:g * A_REP * tq + 2 * tq]
                             + p[g * A_REP * tq + 2 * tq:(g + 1) * A_REP * tq] for g in range(A_KV_HEADS)], axis=0)
    qpos_g = past + _iota((A_KV_HEADS * tq, 1), 0) % tq
    on = _iota((nblk, LANES), 0) * CMP_STRIDE
    os_ = _iota((nblk, LANES), 1) * SEL_BLOCK
    overlap = jnp.where((on < os_ + SEL_BLOCK) & (on + CMP_LEN > os_), 1.0, 0.0).astype(BF16)
    s_i = _iota((1, LANES), 1)
    cur = qpos_g // SEL_BLOCK
    forced = (s_i == 0) | (s_i == cur) | (s_i == cur - 1)
    visible = (s_i * SEL_BLOCK <= qpos_g) & (s_i < n_sel)
    score = jnp.where(visible, _dot_x2(p_sum, overlap) + jnp.where(forced, FORCE_BONUS, 0.0), NEG)
    chosen = jnp.where(_rank_before(score, n_sel) < min(SEL_TOP, n_sel), 1.0, 0.0)
    expand = jnp.where((_iota((LANES, past), 1) // SEL_BLOCK) == _iota((LANES, past), 0), 1.0, 0.0).astype(BF16)
    ck = jnp.dot(chosen.astype(BF16), expand, preferred_element_type=F32)
    sb1 = per_group(jnp.where(ck > 0.5, 0.0, NEG))
    tnew = _iota((1, LANES), 1) - (LANES - seq)
    newok = (tnew >= 0) & (tnew <= trow) & (trow - tnew <= WINDOW)
    new_sel = per_group(chosen[:, new_blk:new_blk + 1]) > 0.5
    sb2 = jnp.where(newok & new_sel, 0.0, NEG)
    o_sel = _softmax_pv2(qar_bd, sel_scr[0:half, :], sel_scr[half:, :], sb1, stail_ref[0:half, :], stail_ref[half:, :], sb2)
    kpos_w = past - wb + _iota((1, wb), 1)
    wb1 = jnp.where((kpos_w <= qpos) & (qpos - kpos_w <= WINDOW), 0.0, NEG)
    wb2 = jnp.where(newok, 0.0, NEG)
    o_win = _softmax_pv2(qar_bd, wcache_ref[0:half, :], wcache_ref[half:, :], wb1,
                         wtail_ref[0:half, :], wtail_ref[half:, :], wb2)
    outs = []
    for h in range(A_HEADS):
        g = h // A_REP
        rs = slice(h * tq, (h + 1) * tq)
        ls = slice(g * HEAD_DIM, (g + 1) * HEAD_DIM)
        c0 = 3 * h
        outs.append(gs[:, c0:c0 + 1] * o_cmp[rs, ls] + gs[:, c0 + 1:c0 + 2] * o_sel[rs, ls] + gs[:, c0 + 2:c0 + 3] * o_win[rs, ls])
    o_ref[...] = jnp.concatenate(outs, axis=1)[0:seq, :]
    rolled = pltpu.roll(wcache_ref[...], wb - seq, 1)
    wstate_ref[:, 0:wb - LANES] = rolled[:, 0:wb - LANES]
    lane = _iota((1, LANES), 1)
    wstate_ref[:, wb - LANES:wb] = jnp.where(lane >= LANES - seq, wtail_ref[...], rolled[:, wb - LANES:wb])


def _nsa_sample(qa, qar, gate, stail, wtail, cmp_pages_t, sel_pages_t, win_cache_t, page_table, cw,
                *, layer, depth, batch, seq):
    pa, pb, pet, poolt, w1t, w2t = cw
    n_pages = page_table.shape[1]
    page = cmp_pages_t.shape[2]
    n_pool = cmp_pages_t.shape[0] // depth
    past = n_pages * page
    wb = win_cache_t.shape[2]
    total = past + seq
    nblk = past // CMP_STRIDE
    assert past % CMP_STRIDE == 0 and seq < CMP_STRIDE and past >= CMP_LEN and seq <= 8
    assert (total - CMP_LEN) // CMP_STRIDE + 1 == nblk - 1 and nblk <= LANES
    assert -(-total // SEL_BLOCK) <= LANES and past % SEL_BLOCK == 0 and page % LANES == 0
    assert wb == WINDOW and wb % LANES == 0 and wb > LANES
    base = layer * n_pool

    def page_spec(j):
        return pl.BlockSpec((None, KV_W, page), lambda b, pt: (base + pt[b, j], 0, 0))

    def tok_spec(wd):
        return pl.BlockSpec((None, seq, wd), lambda b, pt: (b, 0, 0))

    def full_spec(a):
        return pl.BlockSpec(a.shape, lambda b, pt: (0,) * a.ndim)

    tail_spec = pl.BlockSpec((None, KV_W, LANES), lambda b, pt: (b, 0, 0))
    kern = functools.partial(_nsa_sample_kernel, n_pages=n_pages, page=page, seq=seq, past=past, wb=wb)
    r3 = lambda a: a.reshape(batch, seq, a.shape[-1])
    o, wstate = pl.pallas_call(
        kern,
        grid_spec=pltpu.PrefetchScalarGridSpec(
            num_scalar_prefetch=1,
            grid=(batch,),
            in_specs=[page_spec(j) for j in range(n_pages)] + [page_spec(j) for j in range(n_pages)]
            + [pl.BlockSpec((None, KV_W, wb), lambda b, pt: (layer * batch + b, 0, 0)),
               tok_spec(A_WIDTH), tok_spec(A_WIDTH), tok_spec(GATE_PAD), tail_spec, tail_spec,
               full_spec(pa), full_spec(pb), full_spec(pet), full_spec(poolt), full_spec(w1t), full_spec(w2t)],
            out_specs=[tok_spec(A_WIDTH), pl.BlockSpec((None, KV_W, wb), lambda b, pt: (b, 0, 0))],
            scratch_shapes=[pltpu.VMEM((KV_W, past), F32), pltpu.VMEM((KV_W, past), F32),
                            pltpu.VMEM((3, 8, A_WIDTH), F32)]),
        out_shape=[jax.ShapeDtypeStruct((batch, seq, A_WIDTH), F32),
                   jax.ShapeDtypeStruct((batch, KV_W, wb), F32)],
        compiler_params=_cp("parallel"),
        name="nsa_sample",
    )(page_table, *([cmp_pages_t] * n_pages), *([sel_pages_t] * n_pages), win_cache_t,
      r3(qa), r3(qar), r3(gate), stail, wtail, pa, pb, pet, poolt, w1t, w2t)
    return o.reshape(batch * seq, A_WIDTH), wstate


def _moba_means_t(kt, n_keys):
    length = kt.shape[1]
    tb = _iota((length, LANES), 0) // MOBA_BLOCK
    nb = _iota((length, LANES), 1)
    bm = jnp.where((tb == nb) & (_iota((length, LANES), 0) < n_keys), 1.0 / MOBA_BLOCK, 0.0).astype(BF16)
    return _dot_x2(kt, bm)


def _moba_choose(qh, mt, past_ok, n_blocks):
    q_hi, q_lo = _split2(qh)
    m_hi, m_lo = _split2(mt)
    gate = (jnp.dot(q_hi, m_hi, preferred_element_type=F32) + jnp.dot(q_lo, m_hi, preferred_element_type=F32)
            + jnp.dot(q_hi, m_lo, preferred_element_type=F32))
    gate = jnp.where(past_ok, gate, NEG)
    return jnp.where(past_ok & (_rank_before(gate, n_blocks) < min(MOBA_TOP, n_blocks)), 1.0, 0.0)


def _moba_prompt_kernel(q_ref, kvt_ref, o_ref, means_scr, ch_scr, *, seq, tq, n_blocks, kchunk):
    @pl.when(pl.program_id(1) == 0)
    def _():
        means_scr[...] = _moba_means_t(kvt_ref[0:C_WIDTH, :], seq)

    q0 = pl.program_id(1) * tq
    qpos = q0 + _iota((tq, 1), 0)
    q = q_ref[...]
    q_blk = qpos // MOBA_BLOCK
    past_ok = _iota((1, LANES), 1) < q_blk
    for h in range(C_HEADS):
        hs = slice(h * HEAD_DIM, (h + 1) * HEAD_DIM)
        ch_scr[h] = _moba_choose(q[:, hs], means_scr[hs, :], past_ok, n_blocks)
    bucket = (q0 + tq - 1) // kchunk
    for k in range(seq // kchunk):
        @pl.when(bucket == k)
        def _(k=k):
            klen = (k + 1) * kchunk
            expand = jnp.where((_iota((LANES, klen), 1) // MOBA_BLOCK) == _iota((LANES, klen), 0), 1.0, 0.0).astype(BF16)
            kpos = _iota((1, klen), 1)
            own_bias = jnp.where((kpos // MOBA_BLOCK) == q_blk, jnp.where(kpos <= qpos, 0.0, NEG), NEG)
            outs = []
            for h in range(C_HEADS):
                hs = slice(h * HEAD_DIM, (h + 1) * HEAD_DIM)
                vs = slice(C_WIDTH + h * HEAD_DIM, C_WIDTH + (h + 1) * HEAD_DIM)
                ck = jnp.dot(ch_scr[h].astype(BF16), expand, preferred_element_type=F32)
                bias = jnp.where(ck > 0.5, 0.0, own_bias)
                outs.append(_softmax_pv(q[:, hs], kvt_ref[hs, 0:klen], kvt_ref[vs, 0:klen], bias))
            o_ref[...] = jnp.concatenate(outs, axis=1)


def _moba_prompt(cq, kvt, *, batch, seq):
    tq = MOBA_Q_TILE if seq % MOBA_Q_TILE == 0 else Q_TILE
    nq = seq // tq
    n_blocks = -(-seq // MOBA_BLOCK)
    kchunk = min(KEY_CHUNK, seq)
    assert seq % tq == 0 and n_blocks <= LANES and seq % kchunk == 0 and kchunk % tq == 0
    kern = functools.partial(_moba_prompt_kernel, seq=seq, tq=tq, n_blocks=n_blocks, kchunk=kchunk)
    return pl.pallas_call(
        kern,
        grid=(batch, nq),
        in_specs=[pl.BlockSpec((tq, C_WIDTH), lambda b, i: (b * nq + i, 0)),
                  pl.BlockSpec((None, 2 * C_WIDTH, seq), lambda b, i: (b, 0, 0))],
        out_specs=pl.BlockSpec((tq, C_WIDTH), lambda b, i: (b * nq + i, 0)),
        out_shape=jax.ShapeDtypeStruct((batch * seq, C_WIDTH), F32),
        scratch_shapes=[pltpu.VMEM((C_WIDTH, LANES), F32), pltpu.VMEM((C_HEADS, tq, LANES), F32)],
        compiler_params=_cp("parallel", "arbitrary"),
        name="moba_prompt",
    )(cq, kvt)


def _moba_sample_kernel(pt_ref, *refs, n_pages, page, seq, past):
    pages = refs[:n_pages]
    q_ref, tail_ref, o_ref, kv_scr, q_scr = refs[n_pages:]
    del pt_ref
    for j in range(n_pages):
        kv_scr[:, j * page:(j + 1) * page] = pages[j][...]
    tq = q_scr.shape[0]
    q_scr[...] = jnp.zeros_like(q_scr)
    q_scr[0:seq, :] = q_ref[...]
    rows = C_HEADS * tq
    q_bd = _tile_rows(q_scr[...], C_HEADS)
    diag = (_iota((rows, C_WIDTH), 0) // tq) == (_iota((rows, C_WIDTH), 1) // HEAD_DIM)
    q_bd = jnp.where(diag, q_bd, 0.0)
    n_blocks = -(-(past + seq) // MOBA_BLOCK)
    means = _moba_means_t(kv_scr[0:C_WIDTH, :], past)
    trow = _iota((rows, 1), 0) % tq
    qpos = past + trow
    past_ok = _iota((1, LANES), 1) < (qpos // MOBA_BLOCK)
    expand = jnp.where((_iota((LANES, past), 1) // MOBA_BLOCK) == _iota((LANES, past), 0), 1.0, 0.0).astype(BF16)
    tnew = _iota((1, LANES), 1) - (LANES - seq)
    b2 = jnp.where((tnew >= 0) & (tnew <= trow), 0.0, NEG)
    ch = _moba_choose(q_bd, means, past_ok, n_blocks)
    b1 = jnp.where(jnp.dot(ch.astype(BF16), expand, preferred_element_type=F32) > 0.5, 0.0, NEG)
    o_all = _softmax_pv2(q_bd, kv_scr[0:C_WIDTH, :], kv_scr[C_WIDTH:, :], b1,
                         tail_ref[0:C_WIDTH, :], tail_ref[C_WIDTH:, :], b2)
    o_all = jnp.where(diag, o_all, 0.0)
    o = o_all[0:tq]
    for h in range(1, C_HEADS):
        o = o + o_all[h * tq:(h + 1) * tq]
    o_ref[...] = o[0:seq, :]


def _moba_sample(cq, tail, pages_t, page_table, *, layer, depth, batch, seq):
    n_pages = page_table.shape[1]
    page = pages_t.shape[2]
    n_pool = pages_t.shape[0] // depth
    past = n_pages * page
    assert past % MOBA_BLOCK == 0 and seq <= 8 and -(-(past + seq) // MOBA_BLOCK) <= LANES and page % LANES == 0
    base = layer * n_pool
    kern = functools.partial(_moba_sample_kernel, n_pages=n_pages, page=page, seq=seq, past=past)
    o = pl.pallas_call(
        kern,
        grid_spec=pltpu.PrefetchScalarGridSpec(
            num_scalar_prefetch=1,
            grid=(batch,),
            in_specs=[pl.BlockSpec((None, 2 * C_WIDTH, page), (lambda b, pt, j=j: (base + pt[b, j], 0, 0)))
                      for j in range(n_pages)]
            + [pl.BlockSpec((None, seq, C_WIDTH), lambda b, pt: (b, 0, 0)),
               pl.BlockSpec((None, 2 * C_WIDTH, LANES), lambda b, pt: (b, 0, 0))],
            out_specs=pl.BlockSpec((None, seq, C_WIDTH), lambda b, pt: (b, 0, 0)),
            scratch_shapes=[pltpu.VMEM((2 * C_WIDTH, past), F32), pltpu.VMEM((8, C_WIDTH), F32)]),
        out_shape=jax.ShapeDtypeStruct((batch, seq, C_WIDTH), F32),
        compiler_params=_cp("parallel"),
        name="moba_sample",
    )(page_table, *([pages_t] * n_pages), cq.reshape(batch, seq, C_WIDTH), tail)
    return o.reshape(batch * seq, C_WIDTH)


FF_CHUNK = 256


def _ffn_kernel(*refs, seq, tm, d_ff, paged_prev, tiles_per_seq):
    if paged_prev:
        (x_ref, oa_ref, ob_ref, oc_ref, g1_ref, sc2_ref, sh2_ref, g2_ref, n2_ref, wout_ref, wup_ref,
         cw_ref, cb_ref, wdn_ref, p0_ref, p1_ref, y_ref, u_ref, ubuf, acc_scr) = refs
    else:
        (x_ref, oa_ref, ob_ref, oc_ref, g1_ref, sc2_ref, sh2_ref, g2_ref, n2_ref, wout_ref, wup_ref,
         cw_ref, cb_ref, wdn_ref, y_ref, cst_ref, ubuf, acc_scr, carry) = refs
    i = pl.program_id(0)
    mix = (_dot(oa_ref[...], wout_ref[0:A_WIDTH, :])
           + _dot(ob_ref[...], wout_ref[A_WIDTH:A_WIDTH + B_W, :])
           + _dot(oc_ref[...], wout_ref[A_WIDTH + B_W:, :]))
    x1 = x_ref[...] + g1_ref[...] * mix
    h = x1 * lax.rsqrt(jnp.mean(x1 * x1, axis=-1, keepdims=True) + EPS) * n2_ref[...]
    h = (h * (1.0 + sc2_ref[...]) + sh2_ref[...]).astype(BF16)
    acc_scr[...] = jnp.zeros_like(acc_scr)
    if paged_prev:
        trow = _iota((tm, 1), 0) % seq
    else:
        @pl.when(i % tiles_per_seq == 0)
        def _():
            carry[...] = jnp.zeros_like(carry)
    tf = FF_CHUNK
    hdr = 8

    def conv_half(c0):
        c0 = pl.multiple_of(c0, tf)
        u = jnp.dot(h, wup_ref[:, pl.ds(c0, tf)], preferred_element_type=F32)
        ubuf[hdr:hdr + tm, :] = u
        if paged_prev:
            u_ref[:, pl.ds(c0, tf)] = u
            p0 = p0_ref[:, pl.ds(c0, tf)]
            p1 = p1_ref[:, pl.ds(c0, tf)]
            u1 = jnp.where(trow == 0, p1, ubuf[hdr - 1:hdr - 1 + tm, :])
            u2 = jnp.where(trow == 0, p0, jnp.where(trow == 1, p1, ubuf[hdr - 2:hdr - 2 + tm, :]))
        else:
            ubuf[hdr - 2:hdr, :] = carry[:, pl.ds(c0, tf)]
            u1 = ubuf[hdr - 1:hdr - 1 + tm, :]
            u2 = ubuf[hdr - 2:hdr - 2 + tm, :]
            carry[:, pl.ds(c0, tf)] = u[tm - 2:tm, :]
            cst_ref[:, pl.ds(c0, tf)] = u[tm - 2:tm, :]
        w = cw_ref[:, pl.ds(c0, tf)]
        return cb_ref[:, pl.ds(c0, tf)] + w[0:1, :] * u2 + w[1:2, :] * u1 + w[2:3, :] * u

    def chunk(c, carry_):
        f0 = c * tf
        a = conv_half(f0)
        gte = conv_half(d_ff + f0)
        act = (_silu(a) * gte).astype(BF16)
        acc_scr[...] += jnp.dot(act, wdn_ref[pl.ds(pl.multiple_of(f0, tf), tf), :], preferred_element_type=F32)
        return carry_

    lax.fori_loop(0, d_ff // tf, chunk, 0, unroll=True)
    y_ref[...] = x1 + g2_ref[...] * acc_scr[...]


def _ffn(x2, oa, ob, oc, mods, n2, wout_bf, wup_bf, conv_w, conv_b, wdn_bf, prev, *, tm, seq, rows_per_mod):
    n, d = x2.shape
    d_ff = wdn_bf.shape[0]
    assert d_ff % FF_CHUNK == 0 and n % tm == 0
    if rows_per_mod > 1:
        per = rows_per_mod // tm
        mod_spec = pl.BlockSpec((None, 1, d), lambda i: (i // per, 0, 0))
    else:
        mod_spec = pl.BlockSpec((tm, d), lambda i: (i, 0))
    row = lambda wd: pl.BlockSpec((tm, wd), lambda i: (i, 0))
    full = lambda a: pl.BlockSpec(a.shape, lambda i: (0,) * a.ndim)
    in_specs = [row(d), row(A_WIDTH), row(B_W), row(C_WIDTH), mod_spec, mod_spec, mod_spec, mod_spec,
                full(n2), full(wout_bf), full(wup_bf), full(conv_w), full(conv_b), full(wdn_bf)]
    args = [x2, oa, ob, oc, *mods, n2, wout_bf, wup_bf, conv_w, conv_b, wdn_bf]
    scratch = [pltpu.VMEM((tm + 8, FF_CHUNK), F32), pltpu.VMEM((tm, d), F32)]
    if prev is None:
        assert seq % tm == 0 and tm >= 2
        tiles_per_seq = seq // tm
        out_specs = [row(d), pl.BlockSpec((None, CONV_W - 1, 2 * d_ff), lambda i: (i // tiles_per_seq, 0, 0))]
        out_shape = [jax.ShapeDtypeStruct((n, d), F32), jax.ShapeDtypeStruct((n // seq, CONV_W - 1, 2 * d_ff), F32)]
        scratch.append(pltpu.VMEM((CONV_W - 1, 2 * d_ff), F32))
        sem = "arbitrary"
    else:
        assert tm % seq == 0 and seq >= 2
        tiles_per_seq = 1
        in_specs += [row(2 * d_ff), row(2 * d_ff)]
        args += list(prev)
        out_specs = [row(d), row(2 * d_ff)]
        out_shape = [jax.ShapeDtypeStruct((n, d), F32), jax.ShapeDtypeStruct((n, 2 * d_ff), F32)]
        sem = "parallel"
    kern = functools.partial(_ffn_kernel, seq=seq, tm=tm, d_ff=d_ff, paged_prev=prev is not None,
                             tiles_per_seq=tiles_per_seq)
    return pl.pallas_call(
        kern, grid=(n // tm,), in_specs=in_specs, out_specs=out_specs, out_shape=out_shape,
        scratch_shapes=scratch, compiler_params=_cp(sem), name="outproj_convffn",
    )(*args)


def _final_norm_kernel(x_ref, g_ref, o_ref):
    x = x_ref[...]
    o_ref[...] = x * lax.rsqrt(jnp.mean(x * x, axis=-1, keepdims=True) + EPS) * g_ref[...]


def _final_norm(x2, g, tm):
    n, d = x2.shape
    return pl.pallas_call(
        _final_norm_kernel, grid=(n // tm,),
        in_specs=[pl.BlockSpec((tm, d), lambda i: (i, 0)), pl.BlockSpec((1, d), lambda i: (0, 0))],
        out_specs=pl.BlockSpec((tm, d), lambda i: (i, 0)),
        out_shape=jax.ShapeDtypeStruct((n, d), F32),
        compiler_params=_cp("parallel"), name="final_norm",
    )(x2, g)


def _rope_tables(pos):
    half = HEAD_DIM // 2
    inv = ROPE_THETA ** (-jnp.arange(half, dtype=F32) / half)
    ang = pos.astype(F32)[:, None] * inv[None, :]
    cos = jnp.cos(ang)
    sin = jnp.sin(ang)
    return jnp.tile(cos, (1, 4)), jnp.concatenate([-sin, sin, -sin, sin], axis=1), cos.T, sin.T


def _split_w_in(w_in):
    c = 0
    aq = w_in[..., c:c + A_WIDTH]; c += A_WIDTH
    kv = w_in[..., c:c + 3 * KV_W]; c += 3 * KV_W
    gate = w_in[..., c:c + 3 * A_HEADS]; c += 3 * A_HEADS
    hg = w_in[..., c:c + 4 * B_W]; c += 4 * B_W
    cq = w_in[..., c:c + C_WIDTH]; c += C_WIDTH
    ckv = w_in[..., c:c + 2 * C_WIDTH]
    pad = jnp.zeros(w_in.shape[:2] + (GATE_PAD - 3 * A_HEADS,), w_in.dtype)
    w_row = jnp.concatenate([aq, gate, pad, hg, cq], axis=-1).astype(BF16)
    w_kvt = jnp.swapaxes(jnp.concatenate([kv, ckv], axis=-1), 1, 2).astype(BF16)
    return w_row, w_kvt


def _compress_weights(pe, pool, w1, w2, length):
    rep_t = lambda a: jnp.concatenate([a[0], a[0], a[1], a[1]], axis=-1).T
    pet, poolt = rep_t(pe), rep_t(pool)
    pa = jnp.tile(poolt[:, :CMP_STRIDE], (1, length // CMP_STRIDE))
    pb = jnp.tile(poolt[:, CMP_STRIDE:], (1, length // CMP_STRIDE))
    z1 = jnp.zeros_like(w1[0])
    z2 = jnp.zeros_like(w2[0])
    w1bd = jnp.concatenate([jnp.concatenate([w1[0], z1, z1, z1], 1), jnp.concatenate([z1, w1[0], z1, z1], 1),
                            jnp.concatenate([z1, z1, w1[1], z1], 1), jnp.concatenate([z1, z1, z1, w1[1]], 1)], 0)
    w2bd = jnp.concatenate([jnp.concatenate([w2[0], z2, z2, z2], 1), jnp.concatenate([z2, w2[0], z2, z2], 1),
                            jnp.concatenate([z2, z2, w2[1], z2], 1), jnp.concatenate([z2, z2, z2, w2[1]], 1)], 0)
    return pa, pb, pet, poolt, w1bd.T.astype(BF16), w2bd.T.astype(BF16)


def _rows_from_t(xt, heads):
    lead = xt.shape[:-2]
    t = xt.shape[-1]
    x = xt.reshape(lead + (2, heads, HEAD_DIM, t))
    return jnp.moveaxis(x, -1, len(lead))


def kernel(x_prompt, x_sample, cache_nsa_cmp_kv, cache_nsa_sel_kv, cache_nsa_win_kv, state_hgrn, cache_moba_kv, state_ffn_conv, page_table, c_prompt, c_sample, w_ada, b_ada, norm1_g, norm2_g, w_in, w_out, cmp_pe, cmp_pool, cmp_w1, cmp_w2, hgrn_lb_logits, hgrn_norm_g, w_up, conv_w, conv_b, w_down, final_g):
    bp, tp, d = x_prompt.shape
    bs, ts, _ = x_sample.shape
    depth = w_in.shape[0]
    d_ff = w_down.shape[1]
    n_pool, page = cache_moba_kv.shape[1], cache_moba_kv.shape[2]
    past = page_table.shape[1] * page
    np_, ns_ = bp * tp, bs * ts

    w_ada_bf = w_ada.astype(BF16)
    w_row, w_kvt = _split_w_in(w_in)
    w_out_bf = w_out.astype(BF16)
    w_up_bf = w_up.astype(BF16)
    w_dn_bf = w_down.astype(BF16)
    ng4 = jnp.tile(hgrn_norm_g, (1, B_HEADS))

    mod = _modulation(jnp.concatenate([c_prompt, c_sample], axis=0), w_ada_bf, b_ada)
    mod = mod.reshape(depth, bp + bs, 6, d)

    cos_p, sin_p, cost_p, sint_p = _rope_tables(jnp.arange(tp))
    cos_s, sin_s, cost_s, sint_s = _rope_tables(past + jnp.arange(ts))
    cos_s, sin_s = jnp.tile(cos_s, (bs, 1)), jnp.tile(sin_s, (bs, 1))
    cost_s, sint_s = jnp.tile(cost_s, (1, bs)), jnp.tile(sint_s, (1, bs))

    def pages_t(c):
        w = c.shape[3] * c.shape[4] * c.shape[5]
        return jnp.transpose(c, (0, 1, 3, 4, 5, 2)).reshape(c.shape[0] * c.shape[1], w, c.shape[2])

    cmp_pages = pages_t(cache_nsa_cmp_kv)
    sel_pages = pages_t(cache_nsa_sel_kv)
    moba_pages = pages_t(cache_moba_kv)
    win_cache = pages_t(cache_nsa_win_kv)

    def tails(xt):
        c = xt.shape[1]
        x = jnp.transpose(xt.reshape(c, bs, ts), (1, 0, 2))
        return jnp.pad(x, ((0, 0), (0, 0), (LANES - ts, 0)))

    tm_p = min(512, tp)
    tm_f = min(512, tp)
    tm_fs = min(128, ns_)
    xp = x_prompt.reshape(np_, d)
    xs = x_sample.reshape(ns_, d)
    st_p, st_s = [], []
    zero_state = jnp.zeros((bp, B_W, B_W), F32)
    w_keep = min(WINDOW, tp)
    for l in range(depth):
        mp = [mod[l, :bp, j].reshape(bp, 1, d) for j in range(6)]
        ms = [jnp.repeat(mod[l, bp:, j], ts, axis=0) for j in range(6)]
        n1 = norm1_g[l].reshape(1, d)
        n2 = norm2_g[l].reshape(1, d)
        cw_p = _compress_weights(cmp_pe[l], cmp_pool[l], cmp_w1[l], cmp_w2[l], tp)
        cw_s = cw_p if past == tp else _compress_weights(cmp_pe[l], cmp_pool[l], cmp_w1[l], cmp_w2[l], past)

        qa, qar, gate, hg, cq, cmpt, selt, wint, mobat = _in_projection(
            xp, mp[1], mp[0], n1, w_row[l], w_kvt[l], cos_p, sin_p, cost_p, sint_p,
            tm=tm_p, seq=tp, rows_per_mod=tp, rows_per_pos=tp)
        o_a = _nsa_prompt(qa, qar, gate, cmpt, selt, wint, cw_p, batch=bp, seq=tp)
        o_b, st_new = _hgrn(hg, zero_state, hgrn_lb_logits, ng4[l:l + 1], layer=l, batch=bp, seq=tp)
        o_c = _moba_prompt(cq, mobat, batch=bp, seq=tp)
        xp, conv_st = _ffn(xp, o_a, o_b, o_c, (mp[2], mp[4], mp[3], mp[5]), n2, w_out_bf[l], w_up_bf[l],
                           conv_w[l], conv_b[l].reshape(1, -1), w_dn_bf[l], None, tm=tm_f, seq=tp, rows_per_mod=tp)
        st_p.append((cmpt, selt, wint[:, :, tp - w_keep:], _blockdiag_to_state(st_new), mobat, conv_st))

        qa, qar, gate, hg, cq, cmpt, selt, wint, mobat = _in_projection(
            xs, ms[1], ms[0], n1, w_row[l], w_kvt[l], cos_s, sin_s, cost_s, sint_s,
            tm=ns_, seq=ns_, rows_per_mod=1, rows_per_pos=ns_)
        o_a, win_state = _nsa_sample(qa, qar, gate, tails(selt), tails(wint), cmp_pages, sel_pages, win_cache,
                                     page_table, cw_s, layer=l, depth=depth, batch=bs, seq=ts)
        o_b, st_new = _hgrn(hg, _state_to_blockdiag(state_hgrn[l]), hgrn_lb_logits, ng4[l:l + 1],
                            layer=l, batch=bs, seq=ts)
        o_c = _moba_sample(cq, tails(mobat), moba_pages, page_table, layer=l, depth=depth, batch=bs, seq=ts)
        prev = state_ffn_conv[l]
        p0 = jnp.repeat(prev[:, 0], ts, axis=0)
        p1 = jnp.repeat(prev[:, 1], ts, axis=0)
        xs, u_s = _ffn(xs, o_a, o_b, o_c, (ms[2], ms[4], ms[3], ms[5]), n2, w_out_bf[l], w_up_bf[l],
                       conv_w[l], conv_b[l].reshape(1, -1), w_dn_bf[l], (p0, p1), tm=tm_fs, seq=ts, rows_per_mod=1)
        full = jnp.concatenate([prev, u_s.reshape(bs, ts, 2 * d_ff)], axis=1)
        new_t = lambda xt: jnp.transpose(xt.reshape(xt.shape[1], bs, ts), (1, 0, 2))
        st_s.append((new_t(cmpt), new_t(selt), win_state, _blockdiag_to_state(st_new), new_t(mobat),
                     full[:, -(CONV_W - 1):]))

    fg = final_g.reshape(1, d)
    y_p = _final_norm(xp, fg, tm_p).reshape(bp, tp, d)
    y_s = _final_norm(xs, fg, ns_).reshape(bs, ts, d)

    def stacked(sts, j):
        return jnp.stack([st[j] for st in sts], axis=0)

    return (y_p, y_s,
            _rows_from_t(stacked(st_p, 0), A_KV_HEADS), _rows_from_t(stacked(st_p, 1), A_KV_HEADS),
            _rows_from_t(stacked(st_p, 2), A_KV_HEADS), stacked(st_p, 3),
            _rows_from_t(stacked(st_p, 4), C_HEADS), stacked(st_p, 5),
            _rows_from_t(stacked(st_s, 0), A_KV_HEADS), _rows_from_t(stacked(st_s, 1), A_KV_HEADS),
            _rows_from_t(stacked(st_s, 2), A_KV_HEADS), stacked(st_s, 3),
            _rows_from_t(stacked(st_s, 4), C_HEADS), stacked(st_s, 5))
```
